```python
import math
import jax, jax.numpy as jnp
from jax import lax
import numpy as np

D_MODEL = 1024
BATCH = 16
SEQ = 4096
DEPTH = 4
DEC_BATCH = 8
DEC_SEQ = 4096
PAST_LEN = 128

HEAD_DIM = 64
A_HEADS = 8
A_KV_HEADS = 2
B_HEADS = 8
MIX_WIDTH = (A_HEADS + B_HEADS) * HEAD_DIM
GRID_W = 64
ROPE_THETA = 10000.0
Q_BLOCK = 128
DILATED_BRANCHES = ((128, 1), (512, 4), (2048, 16))
N_BUCKETS = 32
REL_MAX_DIST = 1024
D_FF = 2816
CONV_WIDTH = 3
EPS = 1e-6
NEG_INF = -1e30
ATTN_SCALE = HEAD_DIM ** -0.5
SPLIT_SIZES = (A_HEADS * HEAD_DIM, A_KV_HEADS * HEAD_DIM, A_KV_HEADS * HEAD_DIM,
               B_HEADS * HEAD_DIM, B_HEADS * HEAD_DIM, B_HEADS * HEAD_DIM)
IN_WIDTH = sum(SPLIT_SIZES)
SPLIT_POINTS = tuple(int(v) for v in np.cumsum(SPLIT_SIZES)[:-1])

kernel_name = "hybrid_gqa_dilated_convffn_encoder"


def _rmsnorm(x, g):
    xf = x.astype(jnp.float32)
    y = xf * lax.rsqrt(jnp.mean(xf * xf, axis=-1, keepdims=True) + EPS)
    return (y * g.astype(jnp.float32)).astype(x.dtype)


def _axial_rope(S, dtype):
    rows = S // GRID_W
    row = jnp.repeat(jnp.arange(rows), GRID_W).astype(jnp.float32)
    col = jnp.tile(jnp.arange(GRID_W), rows).astype(jnp.float32)
    n = HEAD_DIM // 4
    inv = ROPE_THETA ** (-jnp.arange(n, dtype=jnp.float32) / n)
    ang = jnp.concatenate([row[:, None] * inv, col[:, None] * inv], axis=-1)
    return jnp.cos(ang).astype(dtype), jnp.sin(ang).astype(dtype)


def _apply_rope(x, cos, sin):
    xr = x.reshape(x.shape[:-1] + (HEAD_DIM // 2, 2))
    x0, x1 = xr[..., 0], xr[..., 1]
    c = cos[None, :, None, :]
    s = sin[None, :, None, :]
    return jnp.stack([x0 * c - x1 * s, x0 * s + x1 * c], axis=-1).reshape(x.shape)


def _mixer_a(q, k, v):
    B, S = q.shape[:2]
    nq = S // Q_BLOCK
    grp = A_HEADS // A_KV_HEADS
    qb = q.reshape(B, nq, Q_BLOCK, A_KV_HEADS, grp, HEAD_DIM).transpose(1, 0, 2, 3, 4, 5)

    def block(qi):
        s = jnp.einsum('bqkgd,bskd->bkgqs', qi, k, preferred_element_type=jnp.float32) * ATTN_SCALE
        p = jax.nn.softmax(s, axis=-1).astype(v.dtype)
        return jnp.einsum('bkgqs,bskd->bqkgd', p, v)

    o = lax.map(block, qb)
    return o.transpose(1, 0, 2, 3, 4, 5).reshape(B, S, A_HEADS * HEAD_DIM)


def _t5_bucket(rel):
    nb = N_BUCKETS // 2
    max_exact = nb // 2
    ret = jnp.where(rel > 0, nb, 0)
    n = jnp.abs(rel)
    large = max_exact + (jnp.log(jnp.maximum(n, 1).astype(jnp.float32) / max_exact)
                         / math.log(REL_MAX_DIST / max_exact) * (nb - max_exact)).astype(jnp.int32)
    large = jnp.minimum(large, nb - 1)
    return ret + jnp.where(n < max_exact, n, large)


def _dilated_branch(q, k, v, rel_bias, window, dilation):
    B, S, H, D = q.shape
    half = window // (2 * dilation)
    C = half
    L = S // dilation
    nC = -(-L // C)
    Lp = nC * C

    def sub(x):
        return x.reshape(B, L, dilation, H, D).transpose(0, 2, 1, 3, 4)

    qs = jnp.pad(sub(q), ((0, 0), (0, 0), (0, Lp - L), (0, 0), (0, 0))).reshape(B, dilation, nC, C, H, D)

    def windows(x):
        xp = jnp.pad(sub(x), ((0, 0), (0, 0), (C, C + Lp - L), (0, 0), (0, 0)))
        xp = xp.reshape(B, dilation, nC + 2, C, H, D)
        return jnp.concatenate([xp[:, :, :-2], xp[:, :, 1:-1], xp[:, :, 2:]], axis=3)

    kw = windows(k)
    vw = windows(v)
    i = jnp.arange(C)[:, None]
    j = jnp.arange(3 * C)[None, :]
    delta = j - C - i
    key_pos = jnp.arange(nC)[:, None] * C + jnp.arange(3 * C)[None, :] - C
    valid = (key_pos >= 0) & (key_pos < L)
    mask = (jnp.abs(delta) <= half)[None] & valid[:, None, :]
    bias = rel_bias[_t5_bucket(delta * dilation)].astype(jnp.float32).transpose(2, 0, 1)
    s = jnp.einsum('brcqhd,brckhd->brchqk', qs, kw, preferred_element_type=jnp.float32) * ATTN_SCALE + bias
    s = jnp.where(mask[None, None, :, None], s, NEG_INF)
    m = jnp.max(s, axis=-1, keepdims=True)
    e = jnp.exp(s - m)
    den = jnp.sum(e, axis=-1, keepdims=True)
    o = jnp.einsum('brchqk,brckhd->brcqhd', (e / den).astype(v.dtype), vw)
    lse = (m + jnp.log(den))[..., 0]
    o = o.reshape(B, dilation, Lp, H, D)[:, :, :L].transpose(0, 2, 1, 3, 4).reshape(B, S, H, D)
    lse = lse.transpose(0, 1, 2, 4, 3).reshape(B, dilation, Lp, H)[:, :, :L]
    lse = lse.transpose(0, 2, 1, 3).reshape(B, S, H)
    return o, lse


def _mixer_b(q, k, v, rel_bias):
    B, S = q.shape[:2]
    outs, lses = [], []
    for window, dilation in DILATED_BRANCHES:
        o, l = _dilated_branch(q, k, v, rel_bias, window, dilation)
        outs.append(o)
        lses.append(l)
    w = jax.nn.softmax(jnp.stack(lses, axis=0), axis=0)
    o = jnp.sum(w[..., None].astype(q.dtype) * jnp.stack(outs, axis=0), axis=0)
    return o.reshape(B, S, B_HEADS * HEAD_DIM)


def _dwconv_centred(h, w, b):
    S = h.shape[1]
    pad = CONV_WIDTH // 2
    hp = jnp.pad(h, ((0, 0), (pad, pad), (0, 0)))
    out = b
    for t in range(CONV_WIDTH):
        out = out + hp[:, t:t + S] * w[t]
    return out


def _trunk(x, attn_norm, w_in, q_norm, k_norm, rel_bias, w_out, ffn_norm, w_up, conv_w, conv_b, w_down, final_norm):
    B, S, _ = x.shape
    cos, sin = _axial_rope(S, x.dtype)
    for l in range(DEPTH):
        h = _rmsnorm(x, attn_norm[l])
        p = h @ w_in[l]
        qa, ka, va, qb, kb, vb = jnp.split(p, SPLIT_POINTS, axis=-1)
        qa = _apply_rope(_rmsnorm(qa.reshape(B, S, A_HEADS, HEAD_DIM), q_norm[l]), cos, sin)
        ka = _apply_rope(_rmsnorm(ka.reshape(B, S, A_KV_HEADS, HEAD_DIM), k_norm[l]), cos, sin)
        va = va.reshape(B, S, A_KV_HEADS, HEAD_DIM)
        oa = _mixer_a(qa, ka, va)
        ob = _mixer_b(qb.reshape(B, S, B_HEADS, HEAD_DIM), kb.reshape(B, S, B_HEADS, HEAD_DIM),
                      vb.reshape(B, S, B_HEADS, HEAD_DIM), rel_bias)
        x = x + jnp.concatenate([oa, ob], axis=-1) @ w_out[l]
        h2 = _rmsnorm(x, ffn_norm[l])
        u = _dwconv_centred(h2 @ w_up[l], conv_w[l], conv_b[l])
        g, val = jnp.split(u, 2, axis=-1)
        x = x + (jax.nn.silu(g) * val) @ w_down[l]
    return _rmsnorm(x, final_norm)


def setup_inputs(seed: int = 0) -> dict:
    key = jax.random.key(seed)
    ks = jax.random.split(key, 16)
    f32 = jnp.float32
    nrm = lambda k, shape, scale: jax.random.normal(k, shape, f32) * scale
    return {
        "x_prompt": nrm(ks[0], (BATCH, SEQ, D_MODEL), 1.0),
        "x_sample": nrm(ks[1], (DEC_BATCH, DEC_SEQ, D_MODEL), 1.0),
        "attn_norm": 1.0 + nrm(ks[2], (DEPTH, D_MODEL), 0.01),
        "w_in": nrm(ks[3], (DEPTH, D_MODEL, IN_WIDTH), D_MODEL ** -0.5),
        "q_norm": 1.0 + nrm(ks[4], (DEPTH, HEAD_DIM), 0.01),
        "k_norm": 1.0 + nrm(ks[5], (DEPTH, HEAD_DIM), 0.01),
        "rel_bias": nrm(ks[6], (N_BUCKETS, B_HEADS), 0.5),
        "w_out": nrm(ks[7], (DEPTH, MIX_WIDTH, D_MODEL), MIX_WIDTH ** -0.5),
        "ffn_norm": 1.0 + nrm(ks[8], (DEPTH, D_MODEL), 0.01),
        "w_up": nrm(ks[9], (DEPTH, D_MODEL, 2 * D_FF), D_MODEL ** -0.5),
        "conv_w": nrm(ks[10], (DEPTH, CONV_WIDTH, 2 * D_FF), CONV_WIDTH ** -0.5),
        "conv_b": nrm(ks[11], (DEPTH, 2 * D_FF), 0.01),
        "w_down": nrm(ks[12], (DEPTH, D_FF, D_MODEL), D_FF ** -0.5),
        "final_norm": 1.0 + nrm(ks[13], (D_MODEL,), 0.01),
    }


def reference(x_prompt, x_sample, attn_norm, w_in, q_norm, k_norm, rel_bias, w_out, ffn_norm, w_up, conv_w, conv_b, w_down, final_norm):
    y_prompt = _trunk(x_prompt, attn_norm, w_in, q_norm, k_norm, rel_bias, w_out, ffn_norm, w_up, conv_w, conv_b, w_down, final_norm)
    y_sample = _trunk(x_sample, attn_norm, w_in, q_norm, k_norm, rel_bias, w_out, ffn_norm, w_up, conv_w, conv_b, w_down, final_norm)
    return (y_prompt, y_sample)
```

```python
import functools
import math

import jax
import jax.numpy as jnp
from jax import lax
from jax.experimental import pallas as pl
from jax.experimental.pallas import tpu as pltpu

F32 = jnp.float32
BF16 = jnp.bfloat16

HEAD_DIM = 64
A_HEADS = 8
A_KV_HEADS = 2
B_HEADS = 8
GRID_W = 64
ROPE_THETA = 10000.0
DILATED_BRANCHES = ((128, 1), (512, 4), (2048, 16))
N_BUCKETS = 32
REL_MAX_DIST = 1024
CONV_WIDTH = 3
EPS = 1e-6
NEG_INF = -1e30
ATTN_SCALE = HEAD_DIM ** -0.5

LANES = 128
A_WIDTH = A_HEADS * HEAD_DIM
AKV_WIDTH = A_KV_HEADS * HEAD_DIM
B_WIDTH = B_HEADS * HEAD_DIM
N_PAIRS_A = A_WIDTH // LANES
N_PAIRS_B = B_WIDTH // LANES

DILATIONS = tuple(d for _, d in DILATED_BRANCHES)
HALF = DILATED_BRANCHES[0][0] // (2 * DILATED_BRANCHES[0][1])
assert all(w // (2 * d) == HALF for w, d in DILATED_BRANCHES)
QC = 128
KW = QC + 2 * HALF
N_VARIANTS = 3

TM_PROJ = 512
TQ_A = 256
TM_FFN = 512
HALO = 16
VMEM_LIMIT = 56 * 1024 * 1024


def _rms(x, g):
    ms = jnp.mean(x * x, axis=-1, keepdims=True)
    return x * lax.rsqrt(ms + EPS) * g


def _bias_body(tab_ref, out_ref):
    br = pl.program_id(0)
    var = pl.program_id(1)
    dil = jnp.where(br == 0, DILATIONS[0], jnp.where(br == 1, DILATIONS[1], DILATIONS[2]))
    i = lax.broadcasted_iota(jnp.int32, (QC, KW), 0)
    jj = lax.broadcasted_iota(jnp.int32, (QC, KW), 1)
    delta = jj - var * HALF - i
    rel = delta * dil
    nb = N_BUCKETS // 2
    max_exact = nb // 2
    n = jnp.abs(rel)
    large = max_exact + (jnp.log(jnp.maximum(n, 1).astype(F32) / max_exact)
                         / math.log(REL_MAX_DIST / max_exact) * (nb - max_exact)).astype(jnp.int32)
    large = jnp.minimum(large, nb - 1)
    bucket = jnp.where(rel > 0, nb, 0) + jnp.where(n < max_exact, n, large)
    band = jnp.abs(delta) <= HALF
    for h in range(B_HEADS):
        acc = jnp.zeros((QC, KW), F32)
        for b in range(N_BUCKETS):
            acc = jnp.where(bucket == b, tab_ref[b, h], acc)
        out_ref[0, 0, h] = jnp.where(band, acc, NEG_INF)


def _bias_tiles(rel_bias):
    nbr = len(DILATIONS)
    return pl.pallas_call(
        _bias_body,
        grid=(nbr, N_VARIANTS),
        in_specs=[pl.BlockSpec(memory_space=pltpu.SMEM)],
        out_specs=pl.BlockSpec((1, 1, B_HEADS, QC, KW), lambda b, v: (b, v, 0, 0, 0)),
        out_shape=jax.ShapeDtypeStruct((nbr, N_VARIANTS, B_HEADS, QC, KW), F32),
        name="bias_tiles",
    )(rel_bias)


def _inproj_body(x_ref, g_ref, w_ref, gq_ref, gk_ref, cos_ref, sin_ref,
                 qa_ref, ka_ref, vat_ref, qkvb_ref):
    h = _rms(x_ref[0], g_ref[...]).astype(BF16)
    lane = lax.broadcasted_iota(jnp.int32, (1, LANES), 1)
    lo = lane < HEAD_DIM
    even = (lane & 1) == 0
    cos = cos_ref[...]
    sin = sin_ref[...]

    def norm_rope(blk, gain):
        sq = blk * blk
        s_lo = jnp.sum(jnp.where(lo, sq, 0.0), axis=-1, keepdims=True)
        s_hi = jnp.sum(jnp.where(lo, 0.0, sq), axis=-1, keepdims=True)
        r = jnp.where(lo, lax.rsqrt(s_lo * (1.0 / HEAD_DIM) + EPS), lax.rsqrt(s_hi * (1.0 / HEAD_DIM) + EPS))
        y = blk * r * gain
        partner = jnp.where(even, pltpu.roll(y, LANES - 1, 1), pltpu.roll(y, 1, 1))
        return y * cos + partner * sin

    na = A_WIDTH + 2 * AKV_WIDTH
    pa = jnp.dot(h, w_ref[:, 0:na], preferred_element_type=F32)
    for i in range(N_PAIRS_A):
        qa_ref[0, i] = norm_rope(pa[:, LANES * i:LANES * (i + 1)], gq_ref[...]).astype(BF16)
    ka_ref[0] = norm_rope(pa[:, A_WIDTH:A_WIDTH + AKV_WIDTH], gk_ref[...]).astype(BF16)
    vat_ref[0] = pa[:, A_WIDTH + AKV_WIDTH:na].T.astype(BF16)
    pb = jnp.dot(h, w_ref[:, na:na + 3 * B_WIDTH], preferred_element_type=F32)
    qkvb_ref[0, :, 0:B_WIDTH] = pb[:, 0:B_WIDTH] * ATTN_SCALE
    qkvb_ref[0, :, B_WIDTH:3 * B_WIDTH] = pb[:, B_WIDTH:3 * B_WIDTH]


def _in_proj(x, g, w, gq, gk, cos2, sin2):
    B, S, D = x.shape
    tm = TM_PROJ
    nw = w.shape[1]
    return pl.pallas_call(
        _inproj_body,
        grid=(B, S // tm),
        in_specs=[
            pl.BlockSpec((1, tm, D), lambda b, t: (b, t, 0)),
            pl.BlockSpec((1, D), lambda b, t: (0, 0)),
            pl.BlockSpec((D, nw), lambda b, t: (0, 0)),
            pl.BlockSpec((1, LANES), lambda b, t: (0, 0)),
            pl.BlockSpec((1, LANES), lambda b, t: (0, 0)),
            pl.BlockSpec((tm, LANES), lambda b, t: (t, 0)),
            pl.BlockSpec((tm, LANES), lambda b, t: (t, 0)),
        ],
        out_specs=[
            pl.BlockSpec((1, N_PAIRS_A, tm, LANES), lambda b, t: (b, 0, t, 0)),
            pl.BlockSpec((1, tm, AKV_WIDTH), lambda b, t: (b, t, 0)),
            pl.BlockSpec((1, AKV_WIDTH, tm), lambda b, t: (b, 0, t)),
            pl.BlockSpec((1, tm, 3 * B_WIDTH), lambda b, t: (b, t, 0)),
        ],
        out_shape=[
            jax.ShapeDtypeStruct((B, N_PAIRS_A, S, LANES), BF16),
            jax.ShapeDtypeStruct((B, S, AKV_WIDTH), BF16),
            jax.ShapeDtypeStruct((B, AKV_WIDTH, S), BF16),
            jax.ShapeDtypeStruct((B, S, 3 * B_WIDTH), F32),
        ],
        compiler_params=pltpu.CompilerParams(
            dimension_semantics=("arbitrary", "arbitrary"), vmem_limit_bytes=VMEM_LIMIT),
        name="in_proj",
    )(x, g, w, gq, gk, cos2, sin2)


def _mixa_body(q_ref, k_ref, vt_ref, o_ref, st_scr, pt_scr):
    S = k_ref.shape[1]
    rc = 64
    nchunk = S // rc
    lane = lax.broadcasted_iota(jnp.int32, (1, LANES), 1)
    lo = lane < HEAD_DIM
    for i in range(N_PAIRS_A):
        qb = q_ref[0, i]
        halves = []
        for half in range(2):
            qm = jnp.where(lo if half == 0 else jnp.logical_not(lo), qb, jnp.zeros_like(qb))
            st_scr[...] = lax.dot_general(k_ref[0], qm, (((1,), (1,)), ((), ())),
                                          preferred_element_type=F32)

            def max_body(c, mx):
                return jnp.maximum(mx, st_scr[pl.ds(pl.multiple_of(c * rc, rc), rc), :])

            mx = lax.fori_loop(0, nchunk, max_body, jnp.full((rc, st_scr.shape[1]), NEG_INF, F32))
            m = jnp.max(mx, axis=0, keepdims=True)

            def exp_body(c, ls):
                rows = pl.ds(pl.multiple_of(c * rc, rc), rc)
                e = jnp.exp(st_scr[rows, :] - m)
                pt_scr[rows, :] = e.astype(BF16)
                return ls + e

            ls = lax.fori_loop(0, nchunk, exp_body, jnp.zeros((rc, st_scr.shape[1]), F32))
            den = jnp.sum(ls, axis=0, keepdims=True)
            ot = jnp.dot(vt_ref[0], pt_scr[...], preferred_element_type=F32)
            ot = ot / den
            halves.append(ot[HEAD_DIM * half:HEAD_DIM * (half + 1)])
        o_ref[0, :, LANES * i:LANES * (i + 1)] = jnp.concatenate(halves, axis=0).T.astype(BF16)


def _mix_a(qa, ka, vat):
    B, _, S, _ = qa.shape
    tq = TQ_A
    return pl.pallas_call(
        _mixa_body,
        grid=(B, S // tq),
        in_specs=[
            pl.BlockSpec((1, N_PAIRS_A, tq, LANES), lambda b, t: (b, 0, t, 0)),
            pl.BlockSpec((1, S, AKV_WIDTH), lambda b, t: (b, 0, 0)),
            pl.BlockSpec((1, AKV_WIDTH, S), lambda b, t: (b, 0, 0)),
        ],
        out_specs=pl.BlockSpec((1, tq, A_WIDTH), lambda b, t: (b, t, 0)),
        out_shape=jax.ShapeDtypeStruct((B, S, A_WIDTH), BF16),
        scratch_shapes=[pltpu.VMEM((S, tq), F32), pltpu.VMEM((S, tq), BF16)],
        compiler_params=pltpu.CompilerParams(
            dimension_semantics=("arbitrary", "arbitrary"), vmem_limit_bytes=VMEM_LIMIT),
        name="mix_a",
    )(qa, ka, vat)


def _mixb_body(q_ref, k_ref, v_ref, bm_ref, o_ref, oacc, lacc):
    S = q_ref.shape[1]
    lane = lax.broadcasted_iota(jnp.int32, (1, LANES), 1)
    lo = lane < HEAD_DIM
    hi = jnp.logical_not(lo)

    def rows(start, size, d):
        return pl.ds(start, size) if d == 1 else pl.ds(start, size, stride=d)

    for bi, d in enumerate(DILATIONS):
        L = S // d
        nc = L // QC
        shift = nc.bit_length() - 1

        def body(it, carry, bi=bi, d=d, L=L, nc=nc, shift=shift):
            r = lax.shift_right_logical(it, shift)
            c = it & (nc - 1)
            j0 = c * QC
            ws = jnp.clip(j0 - HALF, 0, L - KW)
            var = jnp.where(c == 0, 0, jnp.where(c == nc - 1, 2, 1))
            qidx = rows(r + d * j0, QC, d)
            kidx = rows(r + d * ws, KW, d)
            qc = q_ref[0, qidx, :].astype(BF16)
            kw = k_ref[0, kidx, :].astype(BF16)
            vw = v_ref[0, kidx, :].astype(BF16)
            outs, lses = [], []
            for half, msk in enumerate((lo, hi)):
                qm = jnp.where(msk, qc, jnp.zeros_like(qc))
                s = lax.dot_general(qm, kw, (((1,), (1,)), ((), ())), preferred_element_type=F32)
                s = s + bm_ref[bi, var, half]
                m = jnp.max(s, axis=-1, keepdims=True)
                e = jnp.exp(s - m)
                den = jnp.sum(e, axis=-1, keepdims=True)
                outs.append(jnp.dot(e.astype(BF16), vw, preferred_element_type=F32) / den)
                lses.append(m + jnp.log(den))
            oacc[bi, qidx, :] = jnp.where(lo, outs[0], outs[1])
            lacc[bi, qidx, :] = jnp.where(lo, lses[0], lses[1])
            return carry

        lax.fori_loop(0, d * nc, body, 0)

    mc = 256

    def merge(ci, carry):
        rws = pl.ds(pl.multiple_of(ci * mc, mc), mc)
        ls = [lacc[bi, rws, :] for bi in range(len(DILATIONS))]
        m = functools.reduce(jnp.maximum, ls)
        es = [jnp.exp(l - m) for l in ls]
        num = functools.reduce(lambda a, b: a + b, [e * oacc[bi, rws, :] for bi, e in enumerate(es)])
        den = functools.reduce(lambda a, b: a + b, es)
        o_ref[0, rws, :] = (num / den).astype(BF16)
        return carry

    lax.fori_loop(0, S // mc, merge, 0)


def _mix_b(qkvb, bmt):
    B, S, _ = qkvb.shape
    nbr = len(DILATIONS)
    assert all((S // d) % QC == 0 and S // d >= KW and ((S // d // QC) & (S // d // QC - 1)) == 0 for d in DILATIONS)
    return pl.pallas_call(
        _mixb_body,
        grid=(B, N_PAIRS_B),
        in_specs=[
            pl.BlockSpec((1, S, LANES), lambda b, p: (b, 0, p)),
            pl.BlockSpec((1, S, LANES), lambda b, p: (b, 0, N_PAIRS_B + p)),
            pl.BlockSpec((1, S, LANES), lambda b, p: (b, 0, 2 * N_PAIRS_B + p)),
            pl.BlockSpec((nbr, N_VARIANTS, 2, QC, KW), lambda b, p: (0, 0, p, 0, 0)),
        ],
        out_specs=pl.BlockSpec((1, S, LANES), lambda b, p: (b, 0, p)),
        out_shape=jax.ShapeDtypeStruct((B, S, B_WIDTH), BF16),
        scratch_shapes=[pltpu.VMEM((nbr, S, LANES), F32), pltpu.VMEM((nbr, S, LANES), F32)],
        compiler_params=pltpu.CompilerParams(
            dimension_semantics=("arbitrary", "arbitrary"), vmem_limit_bytes=VMEM_LIMIT),
        name="mix_b",
    )(qkvb, qkvb, qkvb, bmt)


def _outproj_body(x_ref, oa_ref, ob_ref, w_ref, o_ref):
    o = jnp.concatenate([oa_ref[0], ob_ref[0]], axis=-1)
    o_ref[0] = x_ref[0] + jnp.dot(o, w_ref[...], preferred_element_type=F32)


def _out_proj(x, oa, ob, w):
    B, S, D = x.shape
    tm = TM_PROJ
    return pl.pallas_call(
        _outproj_body,
        grid=(B, S // tm),
        in_specs=[
            pl.BlockSpec((1, tm, D), lambda b, t: (b, t, 0)),
            pl.BlockSpec((1, tm, A_WIDTH), lambda b, t: (b, t, 0)),
            pl.BlockSpec((1, tm, B_WIDTH), lambda b, t: (b, t, 0)),
            pl.BlockSpec(w.shape, lambda b, t: (0, 0)),
        ],
        out_specs=pl.BlockSpec((1, tm, D), lambda b, t: (b, t, 0)),
        out_shape=jax.ShapeDtypeStruct((B, S, D), F32),
        compiler_params=pltpu.CompilerParams(
            dimension_semantics=("arbitrary", "arbitrary"), vmem_limit_bytes=VMEM_LIMIT),
        name="out_proj",
    )(x, oa, ob, w)


def _ffn_body(x_ref, xp_ref, xn_ref, g_ref, wg_ref, wv_ref, cwg_ref, cwv_ref, cbg_ref, cbv_ref,
              wd_ref, fn_ref, o_ref, h_scr, acc_scr, *, final):
    t = pl.program_id(1)
    c = pl.program_id(2)
    nt = pl.num_programs(1)
    ncol = pl.num_programs(2)
    tm = x_ref.shape[1]

    @pl.when(c == 0)
    def _():
        h_scr[HALO:HALO + tm] = _rms(x_ref[0], g_ref[...]).astype(BF16)
        hp = _rms(xp_ref[0], g_ref[...]).astype(BF16)
        h_scr[0:HALO] = jnp.where(t == 0, jnp.zeros_like(hp), hp)
        hn = _rms(xn_ref[0], g_ref[...]).astype(BF16)
        h_scr[HALO + tm:2 * HALO + tm] = jnp.where(t == nt - 1, jnp.zeros_like(hn), hn)

    h = h_scr[...]
    rows = tm + 2 * HALO

    def conv(w_ref, cw_ref, cb_ref):
        up = jnp.dot(h, w_ref[...], preferred_element_type=F32)
        prev = pltpu.roll(up, 1, 0)[HALO:HALO + tm]
        nxt = pltpu.roll(up, rows - 1, 0)[HALO:HALO + tm]
        mid = up[HALO:HALO + tm]
        return cb_ref[...] + prev * cw_ref[0:1, :] + mid * cw_ref[1:2, :] + nxt * cw_ref[2:3, :]

    ug = conv(wg_ref, cwg_ref, cbg_ref)
    uv = conv(wv_ref, cwv_ref, cbv_ref)
    act = (ug * (1.0 / (1.0 + jnp.exp(-ug))) * uv).astype(BF16)
    part = jnp.dot(act, wd_ref[...], preferred_element_type=F32)

    @pl.when(c == 0)
    def _():
        acc_scr[...] = x_ref[0] + part

    @pl.when(c > 0)
    def _():
        acc_scr[...] += part

    @pl.when(c == ncol - 1)
    def _():
        y = acc_scr[...]
        o_ref[0] = _rms(y, fn_ref[...]) if final else y


def _ffn(x, g, w_up, conv_w, conv_b, w_down, fn, final):
    B, S, D = x.shape
    tm = TM_FFN
    dff = w_down.shape[0]
    ncol = 2
    cw = dff // ncol
    assert cw % LANES == 0 and S % tm == 0 and tm % HALO == 0
    hb = tm // HALO
    nhb = S // HALO
    return pl.pallas_call(
        functools.partial(_ffn_body, final=final),
        grid=(B, S // tm, ncol),
        in_specs=[
            pl.BlockSpec((1, tm, D), lambda b, t, c: (b, t, 0)),
            pl.BlockSpec((1, HALO, D), lambda b, t, c: (b, jnp.maximum(t * hb - 1, 0), 0)),
            pl.BlockSpec((1, HALO, D), lambda b, t, c: (b, jnp.minimum((t + 1) * hb, nhb - 1), 0)),
            pl.BlockSpec((1, D), lambda b, t, c: (0, 0)),
            pl.BlockSpec((D, cw), lambda b, t, c: (0, c)),
            pl.BlockSpec((D, cw), lambda b, t, c: (0, ncol + c)),
            pl.BlockSpec((CONV_WIDTH, cw), lambda b, t, c: (0, c)),
            pl.BlockSpec((CONV_WIDTH, cw), lambda b, t, c: (0, ncol + c)),
            pl.BlockSpec((1, cw), lambda b, t, c: (0, c)),
            pl.BlockSpec((1, cw), lambda b, t, c: (0, ncol + c)),
            pl.BlockSpec((cw, D), lambda b, t, c: (c, 0)),
            pl.BlockSpec((1, D), lambda b, t, c: (0, 0)),
        ],
        out_specs=pl.BlockSpec((1, tm, D), lambda b, t, c: (b, t, 0)),
        out_shape=jax.ShapeDtypeStruct((B, S, D), F32),
        scratch_shapes=[pltpu.VMEM((tm + 2 * HALO, D), BF16), pltpu.VMEM((tm, D), F32)],
        compiler_params=pltpu.CompilerParams(
            dimension_semantics=("arbitrary", "arbitrary", "arbitrary"), vmem_limit_bytes=VMEM_LIMIT),
        name="ffn",
    )(x, x, x, g, w_up, w_up, conv_w, conv_w, conv_b, conv_b, w_down, fn)


def _rope_tables(S):
    rows = S // GRID_W
    row = jnp.repeat(jnp.arange(rows), GRID_W).astype(F32)
    col = jnp.tile(jnp.arange(GRID_W), rows).astype(F32)
    n = HEAD_DIM // 4
    inv = ROPE_THETA ** (-jnp.arange(n, dtype=F32) / n)
    ang = jnp.concatenate([row[:, None] * inv, col[:, None] * inv], axis=-1)
    lane = jnp.arange(LANES)
    pair = (lane % HEAD_DIM) // 2
    sign = jnp.where(lane % 2 == 0, -1.0, 1.0).astype(F32)
    return jnp.cos(ang)[:, pair], jnp.sin(ang)[:, pair] * sign


def _qa_column_order():
    c = jnp.arange(A_WIDTH)
    tile, half, dim = c // LANES, (c % LANES) // HEAD_DIM, c % HEAD_DIM
    return (tile + (A_HEADS // A_KV_HEADS) * half) * HEAD_DIM + dim


def _trunk(x, p):
    depth = p["w_in"].shape[0]
    for l in range(depth):
        qa, ka, vat, qkvb = _in_proj(x, p["attn_norm"][l], p["w_in"][l], p["gq"][l], p["gk"][l], p["cos"], p["sin"])
        oa = _mix_a(qa, ka, vat)
        ob = _mix_b(qkvb, p["bias"])
        x = _out_proj(x, oa, ob, p["w_out"][l])
        x = _ffn(x, p["ffn_norm"][l], p["w_up"][l], p["conv_w"][l], p["conv_b"][l], p["w_down"][l],
                 p["final_norm"], final=(l == depth - 1))
    return x


def kernel(x_prompt, x_sample, attn_norm, w_in, q_norm, k_norm, rel_bias, w_out, ffn_norm, w_up, conv_w, conv_b, w_down, final_norm):
    depth, d_model, _ = w_in.shape
    order = _qa_column_order()
    w_in_p = jnp.concatenate([w_in[:, :, order], w_in[:, :, A_WIDTH:]], axis=-1).astype(BF16)
    w_out_p = jnp.concatenate([w_out[:, order, :], w_out[:, A_WIDTH:, :]], axis=1).astype(BF16)
    params = {
        "attn_norm": attn_norm.reshape(depth, 1, d_model),
        "w_in": w_in_p,
        "gq": (jnp.tile(q_norm, (1, 2)) * ATTN_SCALE).reshape(depth, 1, LANES),
        "gk": jnp.tile(k_norm, (1, 2)).reshape(depth, 1, LANES),
        "w_out": w_out_p,
        "ffn_norm": ffn_norm.reshape(depth, 1, d_model),
        "w_up": w_up.astype(BF16),
        "conv_w": conv_w,
        "conv_b": conv_b.reshape(depth, 1, -1),
        "w_down": w_down.astype(BF16),
        "final_norm": final_norm.reshape(1, d_model),
        "bias": _bias_tiles(rel_bias),
    }
    outs = []
    for x in (x_prompt, x_sample):
        cos2, sin2 = _rope_tables(x.shape[1])
        outs.append(_trunk(x, dict(params, cos=cos2, sin=sin2)))
    return tuple(outs)
```

```python
import functools
import math

import jax
import jax.numpy as jnp
from jax import lax
from jax.experimental import pallas as pl
from jax.experimental.pallas import tpu as pltpu

F32 = jnp.float32
BF16 = jnp.bfloat16

HEAD_DIM = 64
A_HEADS = 8
A_KV_HEADS = 2
B_HEADS = 8
GRID_W = 64
ROPE_THETA = 10000.0
DILATED_BRANCHES = ((128, 1), (512, 4), (2048, 16))
N_BUCKETS = 32
REL_MAX_DIST = 1024
CONV_WIDTH = 3
EPS = 1e-6
NEG_INF = -1e30
ATTN_SCALE = HEAD_DIM ** -0.5
LOG2E = math.log2(math.e)
Q_SCALE = ATTN_SCALE * LOG2E

LANES = 128
A_WIDTH = A_HEADS * HEAD_DIM
AKV_WIDTH = A_KV_HEADS * HEAD_DIM
B_WIDTH = B_HEADS * HEAD_DIM
N_PAIRS_A = A_WIDTH // LANES
N_PAIRS_B = B_WIDTH // LANES

DILATIONS = tuple(d for _, d in DILATED_BRANCHES)
HALF = DILATED_BRANCHES[0][0] // (2 * DILATED_BRANCHES[0][1])
assert all(w // (2 * d) == HALF for w, d in DILATED_BRANCHES)
QC = 128
KW = QC + 2 * HALF
N_VARIANTS = 3

TM_PROJ = 512
TQ_A = 256
TM_FFN = 512
HALO = 16
ONES_ROWS = 16
VMEM_LIMIT = 56 * 1024 * 1024


def _rms(x, g):
    ms = jnp.mean(x * x, axis=-1, keepdims=True)
    return x * lax.rsqrt(ms + EPS) * g


def _bias_body(tab_ref, out_ref):
    br = pl.program_id(0)
    var = pl.program_id(1)
    dil = jnp.where(br == 0, DILATIONS[0], jnp.where(br == 1, DILATIONS[1], DILATIONS[2]))
    i = lax.broadcasted_iota(jnp.int32, (QC, KW), 0)
    jj = lax.broadcasted_iota(jnp.int32, (QC, KW), 1)
    delta = jj - var * HALF - i
    rel = delta * dil
    nb = N_BUCKETS // 2
    max_exact = nb // 2
    n = jnp.abs(rel)
    large = max_exact + (jnp.log(jnp.maximum(n, 1).astype(F32) / max_exact)
                         / math.log(REL_MAX_DIST / max_exact) * (nb - max_exact)).astype(jnp.int32)
    large = jnp.minimum(large, nb - 1)
    bucket = jnp.where(rel > 0, nb, 0) + jnp.where(n < max_exact, n, large)
    band = jnp.abs(delta) <= HALF
    for h in range(B_HEADS):
        acc = jnp.zeros((QC, KW), F32)
        for b in range(N_BUCKETS):
            acc = jnp.where(bucket == b, tab_ref[b, h], acc)
        out_ref[0, 0, h] = jnp.where(band, acc * LOG2E, NEG_INF)


def _bias_tiles(rel_bias):
    nbr = len(DILATIONS)
    return pl.pallas_call(
        _bias_body,
        grid=(nbr, N_VARIANTS),
        in_specs=[pl.BlockSpec(memory_space=pltpu.SMEM)],
        out_specs=pl.BlockSpec((1, 1, B_HEADS, QC, KW), lambda b, v: (b, v, 0, 0, 0)),
        out_shape=jax.ShapeDtypeStruct((nbr, N_VARIANTS, B_HEADS, QC, KW), F32),
        name="bias_tiles",
    )(rel_bias)


def _inproj_body(x_ref, g_ref, w_ref, gq_ref, gk_ref, cos_ref, sin_ref,
                 qa_ref, ka_ref, vat_ref, qkvb_ref):
    h = _rms(x_ref[0], g_ref[...]).astype(BF16)
    lane = lax.broadcasted_iota(jnp.int32, (1, LANES), 1)
    lo = lane < HEAD_DIM
    even = (lane & 1) == 0
    cos = cos_ref[...]
    sin = sin_ref[...]

    def norm_rope(blk, gain):
        sq = blk * blk
        s_lo = jnp.sum(jnp.where(lo, sq, 0.0), axis=-1, keepdims=True)
        s_hi = jnp.sum(jnp.where(lo, 0.0, sq), axis=-1, keepdims=True)
        r = jnp.where(lo, lax.rsqrt(s_lo * (1.0 / HEAD_DIM) + EPS), lax.rsqrt(s_hi * (1.0 / HEAD_DIM) + EPS))
        y = blk * r * gain
        partner = jnp.where(even, pltpu.roll(y, LANES - 1, 1), pltpu.roll(y, 1, 1))
        return y * cos + partner * sin

    na = A_WIDTH + 2 * AKV_WIDTH
    pa = jnp.dot(h, w_ref[:, 0:na], preferred_element_type=F32)
    for i in range(N_PAIRS_A):
        qa_ref[0, i] = norm_rope(pa[:, LANES * i:LANES * (i + 1)], gq_ref[...]).astype(BF16)
    ka_ref[0] = norm_rope(pa[:, A_WIDTH:A_WIDTH + AKV_WIDTH], gk_ref[...]).astype(BF16)
    vat_ref[0, 0:AKV_WIDTH] = pa[:, A_WIDTH + AKV_WIDTH:na].T.astype(BF16)
    vat_ref[0, AKV_WIDTH:AKV_WIDTH + ONES_ROWS] = jnp.ones((ONES_ROWS, x_ref.shape[1]), BF16)
    pb = jnp.dot(h, w_ref[:, na:na + 3 * B_WIDTH], preferred_element_type=F32)
    qkvb_ref[0, :, 0:B_WIDTH] = pb[:, 0:B_WIDTH] * Q_SCALE
    qkvb_ref[0, :, B_WIDTH:3 * B_WIDTH] = pb[:, B_WIDTH:3 * B_WIDTH]


def _in_proj(x, g, w, gq, gk, cos2, sin2):
    B, S, D = x.shape
    tm = TM_PROJ
    nw = w.shape[1]
    return pl.pallas_call(
        _inproj_body,
        grid=(B, S // tm),
        in_specs=[
            pl.BlockSpec((1, tm, D), lambda b, t: (b, t, 0)),
            pl.BlockSpec((1, D), lambda b, t: (0, 0)),
            pl.BlockSpec((D, nw), lambda b, t: (0, 0)),
            pl.BlockSpec((1, LANES), lambda b, t: (0, 0)),
            pl.BlockSpec((1, LANES), lambda b, t: (0, 0)),
            pl.BlockSpec((tm, LANES), lambda b, t: (t, 0)),
            pl.BlockSpec((tm, LANES), lambda b, t: (t, 0)),
        ],
        out_specs=[
            pl.BlockSpec((1, N_PAIRS_A, tm, LANES), lambda b, t: (b, 0, t, 0)),
            pl.BlockSpec((1, tm, AKV_WIDTH), lambda b, t: (b, t, 0)),
            pl.BlockSpec((1, AKV_WIDTH + ONES_ROWS, tm), lambda b, t: (b, 0, t)),
            pl.BlockSpec((1, tm, 3 * B_WIDTH), lambda b, t: (b, t, 0)),
        ],
        out_shape=[
            jax.ShapeDtypeStruct((B, N_PAIRS_A, S, LANES), BF16),
            jax.ShapeDtypeStruct((B, S, AKV_WIDTH), BF16),
            jax.ShapeDtypeStruct((B, AKV_WIDTH + ONES_ROWS, S), BF16),
            jax.ShapeDtypeStruct((B, S, 3 * B_WIDTH), F32),
        ],
        compiler_params=pltpu.CompilerParams(
            dimension_semantics=("arbitrary", "arbitrary"), vmem_limit_bytes=VMEM_LIMIT),
        name="in_proj",
    )(x, g, w, gq, gk, cos2, sin2)


def _mixa_body(q_ref, k_ref, vt_ref, o_ref, st_scr, pt_scr):
    S = k_ref.shape[1]
    tq = q_ref.shape[2]
    rc = 64
    lane = lax.broadcasted_iota(jnp.int32, (1, LANES), 1)
    lo = lane < HEAD_DIM
    for i in range(N_PAIRS_A):
        qb = q_ref[0, i]
        zero = jnp.zeros_like(qb)
        q2 = jnp.concatenate([jnp.where(lo, qb, zero), jnp.where(lo, zero, qb)], axis=0)
        st = lax.dot_general(k_ref[0], q2, (((1,), (1,)), ((), ())),
                             preferred_element_type=F32)
        st_scr[...] = st
        m = jnp.max(st, axis=0, keepdims=True)

        def exp_body(c, carry):
            rows = pl.ds(pl.multiple_of(c * rc, rc), rc)
            pt_scr[rows, :] = jnp.exp2(st_scr[rows, :] - m).astype(BF16)
            return carry

        lax.fori_loop(0, S // rc, exp_body, 0, unroll=2)
        ot = jnp.dot(vt_ref[0], pt_scr[...], preferred_element_type=F32)
        ot = ot[0:AKV_WIDTH] / ot[AKV_WIDTH:AKV_WIDTH + 1]
        oblk = jnp.concatenate([ot[0:HEAD_DIM, 0:tq], ot[HEAD_DIM:2 * HEAD_DIM, tq:2 * tq]], axis=0)
        o_ref[0, :, LANES * i:LANES * (i + 1)] = oblk.T.astype(BF16)


def _mix_a(qa, ka, vat):
    B, _, S, _ = qa.shape
    tq = TQ_A
    return pl.pallas_call(
        _mixa_body,
        grid=(B, S // tq),
        in_specs=[
            pl.BlockSpec((1, N_PAIRS_A, tq, LANES), lambda b, t: (b, 0, t, 0)),
            pl.BlockSpec((1, S, AKV_WIDTH), lambda b, t: (b, 0, 0)),
            pl.BlockSpec((1, AKV_WIDTH + ONES_ROWS, S), lambda b, t: (b, 0, 0)),
        ],
        out_specs=pl.BlockSpec((1, tq, A_WIDTH), lambda b, t: (b, t, 0)),
        out_shape=jax.ShapeDtypeStruct((B, S, A_WIDTH), BF16),
        scratch_shapes=[pltpu.VMEM((S, 2 * tq), F32), pltpu.VMEM((S, 2 * tq), BF16)],
        compiler_params=pltpu.CompilerParams(
            dimension_semantics=("arbitrary", "arbitrary"), vmem_limit_bytes=VMEM_LIMIT),
        name="mix_a",
    )(qa, ka, vat)


def _mixb_body(q_ref, k_ref, v_ref, bm_ref, o_ref, oacc, lacc):
    S = q_ref.shape[1]
    lane = lax.broadcasted_iota(jnp.int32, (1, LANES), 1)
    lo = lane < HEAD_DIM
    hi = jnp.logical_not(lo)

    def rows(start, size, d):
        return pl.ds(start, size) if d == 1 else pl.ds(start, size, stride=d)

    for bi, d in enumerate(DILATIONS):
        L = S // d
        nc = L // QC
        shift = nc.bit_length() - 1

        def body(it, carry, bi=bi, d=d, L=L, nc=nc, shift=shift):
            r = lax.shift_right_logical(it, shift)
            c = it & (nc - 1)
            j0 = c * QC
            ws = jnp.clip(j0 - HALF, 0, L - KW)
            var = jnp.where(c == 0, 0, jnp.where(c == nc - 1, 2, 1))
            qidx = rows(r + d * j0, QC, d)
            kidx = rows(r + d * ws, KW, d)
            qc = q_ref[0, qidx, :].astype(BF16)
            kw = k_ref[0, kidx, :].astype(BF16)
            vw = v_ref[0, kidx, :].astype(BF16)
            outs, lses = [], []
            for half, msk in enumerate((lo, hi)):
                qm = jnp.where(msk, qc, jnp.zeros_like(qc))
                s = lax.dot_general(qm, kw, (((1,), (1,)), ((), ())), preferred_element_type=F32)
                s = s + bm_ref[bi, var, half]
                m = jnp.max(s, axis=-1, keepdims=True)
                e = jnp.exp2(s - m)
                den = jnp.sum(e, axis=-1, keepdims=True)
                outs.append(jnp.dot(e.astype(BF16), vw, preferred_element_type=F32) / den)
                lses.append(m + jnp.log2(den))
            oacc[bi, qidx, :] = jnp.where(lo, outs[0], outs[1])
            lacc[bi, qidx, :] = jnp.where(lo, lses[0], lses[1])
            return carry

        lax.fori_loop(0, d * nc, body, 0, unroll=4)

    mc = 256

    def merge(ci, carry):
        rws = pl.ds(pl.multiple_of(ci * mc, mc), mc)
        ls = [lacc[bi, rws, :] for bi in range(len(DILATIONS))]
        m = functools.reduce(jnp.maximum, ls)
        es = [jnp.exp2(l - m) for l in ls]
        num = functools.reduce(lambda a, b: a + b, [e * oacc[bi, rws, :] for bi, e in enumerate(es)])
        den = functools.reduce(lambda a, b: a + b, es)
        o_ref[0, rws, :] = (num / den).astype(BF16)
        return carry

    lax.fori_loop(0, S // mc, merge, 0, unroll=2)


def _mix_b(qkvb, bmt):
    B, S, _ = qkvb.shape
    nbr = len(DILATIONS)
    assert all((S // d) % QC == 0 and S // d >= KW and ((S // d // QC) & (S // d // QC - 1)) == 0 for d in DILATIONS)
    return pl.pallas_call(
        _mixb_body,
        grid=(B, N_PAIRS_B),
        in_specs=[
            pl.BlockSpec((1, S, LANES), lambda b, p: (b, 0, p)),
            pl.BlockSpec((1, S, LANES), lambda b, p: (b, 0, N_PAIRS_B + p)),
            pl.BlockSpec((1, S, LANES), lambda b, p: (b, 0, 2 * N_PAIRS_B + p)),
            pl.BlockSpec((nbr, N_VARIANTS, 2, QC, KW), lambda b, p: (0, 0, p, 0, 0)),
        ],
        out_specs=pl.BlockSpec((1, S, LANES), lambda b, p: (b, 0, p)),
        out_shape=jax.ShapeDtypeStruct((B, S, B_WIDTH), BF16),
        scratch_shapes=[pltpu.VMEM((nbr, S, LANES), F32), pltpu.VMEM((nbr, S, LANES), F32)],
        compiler_params=pltpu.CompilerParams(
            dimension_semantics=("arbitrary", "arbitrary"), vmem_limit_bytes=VMEM_LIMIT),
        name="mix_b",
    )(qkvb, qkvb, qkvb, bmt)


def _outproj_body(x_ref, oa_ref, ob_ref, w_ref, o_ref):
    o = jnp.concatenate([oa_ref[0], ob_ref[0]], axis=-1)
    o_ref[0] = x_ref[0] + jnp.dot(o, w_ref[...], preferred_element_type=F32)


def _out_proj(x, oa, ob, w):
    B, S, D = x.shape
    tm = TM_PROJ
    return pl.pallas_call(
        _outproj_body,
        grid=(B, S // tm),
        in_specs=[
            pl.BlockSpec((1, tm, D), lambda b, t: (b, t, 0)),
            pl.BlockSpec((1, tm, A_WIDTH), lambda b, t: (b, t, 0)),
            pl.BlockSpec((1, tm, B_WIDTH), lambda b, t: (b, t, 0)),
            pl.BlockSpec(w.shape, lambda b, t: (0, 0)),
        ],
        out_specs=pl.BlockSpec((1, tm, D), lambda b, t: (b, t, 0)),
        out_shape=jax.ShapeDtypeStruct((B, S, D), F32),
        compiler_params=pltpu.CompilerParams(
            dimension_semantics=("arbitrary", "arbitrary"), vmem_limit_bytes=VMEM_LIMIT),
        name="out_proj",
    )(x, oa, ob, w)


def _ffn_body(x_ref, xp_ref, xn_ref, g_ref, wg_ref, wv_ref, cwg_ref, cwv_ref, cbg_ref, cbv_ref,
              wd_ref, fn_ref, o_ref, h_scr, acc_scr, *, final):
    t = pl.program_id(1)
    c = pl.program_id(2)
    nt = pl.num_programs(1)
    ncol = pl.num_programs(2)
    tm = x_ref.shape[1]

    @pl.when(c == 0)
    def _():
        h_scr[HALO:HALO + tm] = _rms(x_ref[0], g_ref[...]).astype(BF16)
        hp = _rms(xp_ref[0], g_ref[...]).astype(BF16)
        h_scr[0:HALO] = jnp.where(t == 0, jnp.zeros_like(hp), hp)
        hn = _rms(xn_ref[0], g_ref[...]).astype(BF16)
        h_scr[HALO + tm:2 * HALO + tm] = jnp.where(t == nt - 1, jnp.zeros_like(hn), hn)

    h = h_scr[...]
    rows = tm + 2 * HALO

    def conv(w_ref, cw_ref, cb_ref):
        up = jnp.dot(h, w_ref[...], preferred_element_type=F32)
        prev = pltpu.roll(up, 1, 0)[HALO:HALO + tm]
        nxt = pltpu.roll(up, rows - 1, 0)[HALO:HALO + tm]
        mid = up[HALO:HALO + tm]
        return cb_ref[...] + prev * cw_ref[0:1, :] + mid * cw_ref[1:2, :] + nxt * cw_ref[2:3, :]

    ug = conv(wg_ref, cwg_ref, cbg_ref)
    uv = conv(wv_ref, cwv_ref, cbv_ref)
    act = (ug * (1.0 / (1.0 + jnp.exp(-ug))) * uv).astype(BF16)
    part = jnp.dot(act, wd_ref[...], preferred_element_type=F32)

    @pl.when(c == 0)
    def _():
        acc_scr[...] = x_ref[0] + part

    @pl.when(c > 0)
    def _():
        acc_scr[...] += part

    @pl.when(c == ncol - 1)
    def _():
        y = acc_scr[...]
        o_ref[0] = _rms(y, fn_ref[...]) if final else y


def _ffn(x, g, w_up, conv_w, conv_b, w_down, fn, final):
    B, S, D = x.shape
    tm = TM_FFN
    dff = w_down.shape[0]
    ncol = 2
    cw = dff // ncol
    assert cw % LANES == 0 and S % tm == 0 and tm % HALO == 0
    hb = tm // HALO
    nhb = S // HALO
    return pl.pallas_call(
        functools.partial(_ffn_body, final=final),
        grid=(B, S // tm, ncol),
        in_specs=[
            pl.BlockSpec((1, tm, D), lambda b, t, c: (b, t, 0)),
            pl.BlockSpec((1, HALO, D), lambda b, t, c: (b, jnp.maximum(t * hb - 1, 0), 0)),
            pl.BlockSpec((1, HALO, D), lambda b, t, c: (b, jnp.minimum((t + 1) * hb, nhb - 1), 0)),
            pl.BlockSpec((1, D), lambda b, t, c: (0, 0)),
            pl.BlockSpec((D, cw), lambda b, t, c: (0, c)),
            pl.BlockSpec((D, cw), lambda b, t, c: (0, ncol + c)),
            pl.BlockSpec((CONV_WIDTH, cw), lambda b, t, c: (0, c)),
            pl.BlockSpec((CONV_WIDTH, cw), lambda b, t, c: (0, ncol + c)),
            pl.BlockSpec((1, cw), lambda b, t, c: (0, c)),
            pl.BlockSpec((1, cw), lambda b, t, c: (0, ncol + c)),
            pl.BlockSpec((cw, D), lambda b, t, c: (c, 0)),
            pl.BlockSpec((1, D), lambda b, t, c: (0, 0)),
        ],
        out_specs=pl.BlockSpec((1, tm, D), lambda b, t, c: (b, t, 0)),
        out_shape=jax.ShapeDtypeStruct((B, S, D), F32),
        scratch_shapes=[pltpu.VMEM((tm + 2 * HALO, D), BF16), pltpu.VMEM((tm, D), F32)],
        compiler_params=pltpu.CompilerParams(
            dimension_semantics=("arbitrary", "arbitrary", "arbitrary"), vmem_limit_bytes=VMEM_LIMIT),
        name="ffn",
    )(x, x, x, g, w_up, w_up, conv_w, conv_w, conv_b, conv_b, w_down, fn)


def _rope_tables(S):
    rows = S // GRID_W
    row = jnp.repeat(jnp.arange(rows), GRID_W).astype(F32)
    col = jnp.tile(jnp.arange(GRID_W), rows).astype(F32)
    n = HEAD_DIM // 4
    inv = ROPE_THETA ** (-jnp.arange(n, dtype=F32) / n)
    ang = jnp.concatenate([row[:, None] * inv, col[:, None] * inv], axis=-1)
    lane = jnp.arange(LANES)
    pair = (lane % HEAD_DIM) // 2
    sign = jnp.where(lane % 2 == 0, -1.0, 1.0).astype(F32)
    return jnp.cos(ang)[:, pair], jnp.sin(ang)[:, pair] * sign


def _qa_column_order():
    c = jnp.arange(A_WIDTH)
    tile, half, dim = c // LANES, (c % LANES) // HEAD_DIM, c % HEAD_DIM
    return (tile + (A_HEADS // A_KV_HEADS) * half) * HEAD_DIM + dim


def _trunk(x, p):
    depth = p["w_in"].shape[0]
    for l in range(depth):
        qa, ka, vat, qkvb = _in_proj(x, p["attn_norm"][l], p["w_in"][l], p["gq"][l], p["gk"][l], p["cos"], p["sin"])
        oa = _mix_a(qa, ka, vat)
        ob = _mix_b(qkvb, p["bias"])
        x = _out_proj(x, oa, ob, p["w_out"][l])
        x = _ffn(x, p["ffn_norm"][l], p["w_up"][l], p["conv_w"][l], p["conv_b"][l], p["w_down"][l],
                 p["final_norm"], final=(l == depth - 1))
    return x


def kernel(x_prompt, x_sample, attn_norm, w_in, q_norm, k_norm, rel_bias, w_out, ffn_norm, w_up, conv_w, conv_b, w_down, final_norm):
    depth, d_model, _ = w_in.shape
    order = _qa_column_order()
    w_in_p = jnp.concatenate([w_in[:, :, order], w_in[:, :, A_WIDTH:]], axis=-1).astype(BF16)
    w_out_p = jnp.concatenate([w_out[:, order, :], w_out[:, A_WIDTH:, :]], axis=1).astype(BF16)
    params = {
        "attn_norm": attn_norm.reshape(depth, 1, d_model),
        "w_in": w_in_p,
        "gq": (jnp.tile(q_norm, (1, 2)) * Q_SCALE).reshape(depth, 1, LANES),
        "gk": jnp.tile(k_norm, (1, 2)).reshape(depth, 1, LANES),
        "w_out": w_out_p,
        "ffn_norm": ffn_norm.reshape(depth, 1, d_model),
        "w_up": w_up.astype(BF16),
        "conv_w": conv_w,
        "conv_b": conv_b.reshape(depth, 1, -1),
        "w_down": w_down.astype(BF16),
        "final_norm": final_norm.reshape(1, d_model),
        "bias": _bias_tiles(rel_bias),
    }
    outs = []
    for x in (x_prompt, x_sample):
        cos2, sin2 = _rope_tables(x.shape[1])
        outs.append(_trunk(x, dict(params, cos=cos2, sin=sin2)))
    return tuple(outs)
```

```python
import functools
import math

import jax
import jax.numpy as jnp
from jax import lax
from jax.experimental import pallas as pl
from jax.experimental.pallas import tpu as pltpu

F32 = jnp.float32
BF16 = jnp.bfloat16

HEAD_DIM = 64
A_HEADS = 8
A_KV_HEADS = 2
B_HEADS = 8
GRID_W = 64
ROPE_THETA = 10000.0
DILATED_BRANCHES = ((128, 1), (512, 4), (2048, 16))
N_BUCKETS = 32
REL_MAX_DIST = 1024
CONV_WIDTH = 3
EPS = 1e-6
NEG_INF = -1e30
ATTN_SCALE = HEAD_DIM ** -0.5
LOG2E = math.log2(math.e)
Q_SCALE = ATTN_SCALE * LOG2E

LANES = 128
A_WIDTH = A_HEADS * HEAD_DIM
AKV_WIDTH = A_KV_HEADS * HEAD_DIM
B_WIDTH = B_HEADS * HEAD_DIM
N_PAIRS_A = A_WIDTH // LANES
N_PAIRS_B = B_WIDTH // LANES

DILATIONS = tuple(d for _, d in DILATED_BRANCHES)
HALF = DILATED_BRANCHES[0][0] // (2 * DILATED_BRANCHES[0][1])
assert all(w // (2 * d) == HALF for w, d in DILATED_BRANCHES)
QC = 128
KW = QC + 2 * HALF
N_VARIANTS = 3

TM_PROJ = 512
TQ_A = 256
TM_FFN = 512
HALO = 16
ONES_ROWS = 16
VMEM_LIMIT = 56 * 1024 * 1024


def _rms(x, g):
    ms = jnp.mean(x * x, axis=-1, keepdims=True)
    return x * lax.rsqrt(ms + EPS) * g


def _bias_body(tab_ref, out_ref):
    br = pl.program_id(0)
    var = pl.program_id(1)
    dil = jnp.where(br == 0, DILATIONS[0], jnp.where(br == 1, DILATIONS[1], DILATIONS[2]))
    i = lax.broadcasted_iota(jnp.int32, (QC, KW), 0)
    jj = lax.broadcasted_iota(jnp.int32, (QC, KW), 1)
    delta = jj - var * HALF - i
    rel = delta * dil
    nb = N_BUCKETS // 2
    max_exact = nb // 2
    n = jnp.abs(rel)
    large = max_exact + (jnp.log(jnp.maximum(n, 1).astype(F32) / max_exact)
                         / math.log(REL_MAX_DIST / max_exact) * (nb - max_exact)).astype(jnp.int32)
    large = jnp.minimum(large, nb - 1)
    bucket = jnp.where(rel > 0, nb, 0) + jnp.where(n < max_exact, n, large)
    band = jnp.abs(delta) <= HALF
    for h in range(B_HEADS):
        acc = jnp.zeros((QC, KW), F32)
        for b in range(N_BUCKETS):
            acc = jnp.where(bucket == b, tab_ref[b, h], acc)
        out_ref[0, 0, h] = jnp.where(band, acc * LOG2E, NEG_INF)


def _bias_tiles(rel_bias):
    nbr = len(DILATIONS)
    return pl.pallas_call(
        _bias_body,
        grid=(nbr, N_VARIANTS),
        in_specs=[pl.BlockSpec(memory_space=pltpu.SMEM)],
        out_specs=pl.BlockSpec((1, 1, B_HEADS, QC, KW), lambda b, v: (b, v, 0, 0, 0)),
        out_shape=jax.ShapeDtypeStruct((nbr, N_VARIANTS, B_HEADS, QC, KW), F32),
        name="bias_tiles",
    )(rel_bias)


def _inproj_body(x_ref, g_ref, w_ref, gq_ref, gk_ref, cos_ref, sin_ref,
                 qa_ref, ka_ref, vat_ref, qkvb_ref):
    h = _rms(x_ref[0], g_ref[...]).astype(BF16)
    lane = lax.broadcasted_iota(jnp.int32, (1, LANES), 1)
    lo = lane < HEAD_DIM
    even = (lane & 1) == 0
    cos = cos_ref[...]
    sin = sin_ref[...]

    def norm_rope(blk, gain):
        sq = blk * blk
        s_lo = jnp.sum(jnp.where(lo, sq, 0.0), axis=-1, keepdims=True)
        s_hi = jnp.sum(jnp.where(lo, 0.0, sq), axis=-1, keepdims=True)
        r = jnp.where(lo, lax.rsqrt(s_lo * (1.0 / HEAD_DIM) + EPS), lax.rsqrt(s_hi * (1.0 / HEAD_DIM) + EPS))
        y = blk * r * gain
        partner = jnp.where(even, pltpu.roll(y, LANES - 1, 1), pltpu.roll(y, 1, 1))
        return y * cos + partner * sin

    na = A_WIDTH + 2 * AKV_WIDTH
    pa = jnp.dot(h, w_ref[:, 0:na], preferred_element_type=F32)
    for i in range(N_PAIRS_A):
        qa_ref[0, i] = norm_rope(pa[:, LANES * i:LANES * (i + 1)], gq_ref[...]).astype(BF16)
    ka_ref[0] = norm_rope(pa[:, A_WIDTH:A_WIDTH + AKV_WIDTH], gk_ref[...]).astype(BF16)
    vat_ref[0, 0, 0:AKV_WIDTH] = pa[:, A_WIDTH + AKV_WIDTH:na].T.astype(BF16)
    vat_ref[0, 0, AKV_WIDTH:AKV_WIDTH + ONES_ROWS] = jnp.ones((ONES_ROWS, x_ref.shape[1]), BF16)
    pb = jnp.dot(h, w_ref[:, na:na + 3 * B_WIDTH], preferred_element_type=F32)
    qkvb_ref[0, :, 0:B_WIDTH] = pb[:, 0:B_WIDTH] * Q_SCALE
    qkvb_ref[0, :, B_WIDTH:3 * B_WIDTH] = pb[:, B_WIDTH:3 * B_WIDTH]


def _in_proj(x, g, w, gq, gk, cos2, sin2):
    B, S, D = x.shape
    tm = TM_PROJ
    nw = w.shape[1]
    return pl.pallas_call(
        _inproj_body,
        grid=(B, S // tm),
        in_specs=[
            pl.BlockSpec((1, tm, D), lambda b, t: (b, t, 0)),
            pl.BlockSpec((1, D), lambda b, t: (0, 0)),
            pl.BlockSpec((D, nw), lambda b, t: (0, 0)),
            pl.BlockSpec((1, LANES), lambda b, t: (0, 0)),
            pl.BlockSpec((1, LANES), lambda b, t: (0, 0)),
            pl.BlockSpec((tm, LANES), lambda b, t: (t, 0)),
            pl.BlockSpec((tm, LANES), lambda b, t: (t, 0)),
        ],
        out_specs=[
            pl.BlockSpec((1, N_PAIRS_A, tm, LANES), lambda b, t: (b, 0, t, 0)),
            pl.BlockSpec((1, tm, AKV_WIDTH), lambda b, t: (b, t, 0)),
            pl.BlockSpec((1, 1, AKV_WIDTH + ONES_ROWS, tm), lambda b, t: (b, t, 0, 0)),
            pl.BlockSpec((1, tm, 3 * B_WIDTH), lambda b, t: (b, t, 0)),
        ],
        out_shape=[
            jax.ShapeDtypeStruct((B, N_PAIRS_A, S, LANES), BF16),
            jax.ShapeDtypeStruct((B, S, AKV_WIDTH), BF16),
            jax.ShapeDtypeStruct((B, S // tm, AKV_WIDTH + ONES_ROWS, tm), BF16),
            jax.ShapeDtypeStruct((B, S, 3 * B_WIDTH), F32),
        ],
        compiler_params=pltpu.CompilerParams(
            dimension_semantics=("arbitrary", "arbitrary"), vmem_limit_bytes=VMEM_LIMIT),
        name="in_proj",
    )(x, g, w, gq, gk, cos2, sin2)


def _mixa_body(q_ref, k_ref, vt_ref, o_ref, st_a, st_b, acc_scr, q2_scr):
    S = k_ref.shape[1]
    tq = TQ_A
    kc = vt_ref.shape[3]
    n_units = (S // tq) * N_PAIRS_A
    pair_shift = N_PAIRS_A.bit_length() - 1
    lane = lax.broadcasted_iota(jnp.int32, (1, LANES), 1)
    lo = lane < HEAD_DIM
    st_b[...] = jnp.zeros(st_b.shape, F32)

    def stage(s, m_prev, st_cur, st_prv):
        u = jnp.minimum(s, n_units - 1)
        t = lax.shift_right_logical(u, pair_shift)
        i = u & (N_PAIRS_A - 1)
        qb = q_ref[0, i, pl.ds(pl.multiple_of(t * tq, tq), tq), :]
        zero = jnp.zeros_like(qb)
        q2_scr[...] = jnp.concatenate([jnp.where(lo, qb, zero), jnp.where(lo, zero, qb)], axis=0)
        acc_scr[...] = jnp.zeros(acc_scr.shape, F32)

        def chunk(c, mx):
            rows = pl.ds(pl.multiple_of(c * kc, kc), kc)
            e = jnp.exp2(st_prv[rows, :] - m_prev).astype(BF16)
            sn = lax.dot_general(k_ref[0, rows, :], q2_scr[...], (((1,), (1,)), ((), ())),
                                 preferred_element_type=F32)
            st_cur[rows, :] = sn
            acc_scr[...] += jnp.dot(vt_ref[0, c], e, preferred_element_type=F32)
            return jnp.maximum(mx, jnp.max(sn, axis=0, keepdims=True))

        m_cur = lax.fori_loop(0, S // kc, chunk, jnp.full((1, 2 * tq), NEG_INF, F32), unroll=8)

        @pl.when(jnp.logical_and(s >= 1, s <= n_units))
        def _():
            up = s - 1
            tp = lax.shift_right_logical(up, pair_shift)
            ip = up & (N_PAIRS_A - 1)
            ot = acc_scr[...]
            ot = ot[0:AKV_WIDTH] / ot[AKV_WIDTH:AKV_WIDTH + 1]
            oblk = jnp.concatenate([ot[0:HEAD_DIM, 0:tq], ot[HEAD_DIM:2 * HEAD_DIM, tq:2 * tq]], axis=0)
            o_ref[0, ip, pl.ds(pl.multiple_of(tp * tq, tq), tq), :] = oblk.T.astype(BF16)

        return m_cur

    def two_stages(j, m):
        m = stage(2 * j, m, st_a, st_b)
        return stage(2 * j + 1, m, st_b, st_a)

    lax.fori_loop(0, (n_units + 2) // 2, two_stages, jnp.zeros((1, 2 * tq), F32))


def _mix_a(qa, ka, vat):
    B, _, S, _ = qa.shape
    tq = TQ_A
    _, nkc, vrows, kc = vat.shape
    return pl.pallas_call(
        _mixa_body,
        grid=(B,),
        in_specs=[
            pl.BlockSpec((1, N_PAIRS_A, S, LANES), lambda b: (b, 0, 0, 0)),
            pl.BlockSpec((1, S, AKV_WIDTH), lambda b: (b, 0, 0)),
            pl.BlockSpec((1, nkc, vrows, kc), lambda b: (b, 0, 0, 0)),
        ],
        out_specs=pl.BlockSpec((1, N_PAIRS_A, S, LANES), lambda b: (b, 0, 0, 0)),
        out_shape=jax.ShapeDtypeStruct((B, N_PAIRS_A, S, LANES), BF16),
        scratch_shapes=[pltpu.VMEM((S, 2 * tq), F32), pltpu.VMEM((S, 2 * tq), F32), pltpu.VMEM((vrows, 2 * tq), F32),
                        pltpu.VMEM((2 * tq, LANES), BF16)],
        compiler_params=pltpu.CompilerParams(
            dimension_semantics=("arbitrary",), vmem_limit_bytes=VMEM_LIMIT),
        name="mix_a",
    )(qa, ka, vat)


def _mixb_body(q_ref, k_ref, v_ref, bm_ref, o_ref, oacc, lacc):
    S = q_ref.shape[1]
    lane = lax.broadcasted_iota(jnp.int32, (1, LANES), 1)
    lo = lane < HEAD_DIM
    hi = jnp.logical_not(lo)

    def rows(start, size, d):
        return pl.ds(start, size) if d == 1 else pl.ds(start, size, stride=d)

    for bi, d in enumerate(DILATIONS):
        L = S // d
        nc = L // QC
        shift = nc.bit_length() - 1

        def body(it, carry, bi=bi, d=d, L=L, nc=nc, shift=shift):
            r = lax.shift_right_logical(it, shift)
            c = it & (nc - 1)
            j0 = c * QC
            ws = jnp.clip(j0 - HALF, 0, L - KW)
            var = jnp.where(c == 0, 0, jnp.where(c == nc - 1, 2, 1))
            qidx = rows(r + d * j0, QC, d)
            kidx = rows(r + d * ws, KW, d)
            qc = q_ref[0, qidx, :].astype(BF16)
            kw = k_ref[0, kidx, :].astype(BF16)
            vw = v_ref[0, kidx, :].astype(BF16)
            outs, lses = [], []
            for half, msk in enumerate((lo, hi)):
                qm = jnp.where(msk, qc, jnp.zeros_like(qc))
                s = lax.dot_general(qm, kw, (((1,), (1,)), ((), ())), preferred_element_type=F32)
                s = s + bm_ref[bi, var, half]
                m = jnp.max(s, axis=-1, keepdims=True)
                e = jnp.exp2(s - m)
                den = jnp.sum(e, axis=-1, keepdims=True)
                outs.append(jnp.dot(e.astype(BF16), vw, preferred_element_type=F32) / den)
                lses.append(m + jnp.log2(den))
            oacc[bi, qidx, :] = jnp.where(lo, outs[0], outs[1])
            lacc[bi, qidx, :] = jnp.where(lo, lses[0], lses[1])
            return carry

        lax.fori_loop(0, d * nc, body, 0, unroll=4)

    mc = 256

    def merge(ci, carry):
        rws = pl.ds(pl.multiple_of(ci * mc, mc), mc)
        ls = [lacc[bi, rws, :] for bi in range(len(DILATIONS))]
        m = functools.reduce(jnp.maximum, ls)
        es = [jnp.exp2(l - m) for l in ls]
        num = functools.reduce(lambda a, b: a + b, [e * oacc[bi, rws, :] for bi, e in enumerate(es)])
        den = functools.reduce(lambda a, b: a + b, es)
        o_ref[0, rws, :] = (num / den).astype(BF16)
        return carry

    lax.fori_loop(0, S // mc, merge, 0, unroll=2)


def _mix_b(qkvb, bmt):
    B, S, _ = qkvb.shape
    nbr = len(DILATIONS)
    assert all((S // d) % QC == 0 and S // d >= KW and ((S // d // QC) & (S // d // QC - 1)) == 0 for d in DILATIONS)
    return pl.pallas_call(
        _mixb_body,
        grid=(B, N_PAIRS_B),
        in_specs=[
            pl.BlockSpec((1, S, LANES), lambda b, p: (b, 0, p)),
            pl.BlockSpec((1, S, LANES), lambda b, p: (b, 0, N_PAIRS_B + p)),
            pl.BlockSpec((1, S, LANES), lambda b, p: (b, 0, 2 * N_PAIRS_B + p)),
            pl.BlockSpec((nbr, N_VARIANTS, 2, QC, KW), lambda b, p: (0, 0, p, 0, 0)),
        ],
        out_specs=pl.BlockSpec((1, S, LANES), lambda b, p: (b, 0, p)),
        out_shape=jax.ShapeDtypeStruct((B, S, B_WIDTH), BF16),
        scratch_shapes=[pltpu.VMEM((nbr, S, LANES), F32), pltpu.VMEM((nbr, S, LANES), F32)],
        compiler_params=pltpu.CompilerParams(
            dimension_semantics=("arbitrary", "arbitrary"), vmem_limit_bytes=VMEM_LIMIT),
        name="mix_b",
    )(qkvb, qkvb, qkvb, bmt)


def _outproj_body(x_ref, oa_ref, ob_ref, w_ref, o_ref):
    o = jnp.concatenate([oa_ref[0, i] for i in range(N_PAIRS_A)] + [ob_ref[0]], axis=-1)
    o_ref[0] = x_ref[0] + jnp.dot(o, w_ref[...], preferred_element_type=F32)


def _out_proj(x, oa, ob, w):
    B, S, D = x.shape
    tm = TM_PROJ
    return pl.pallas_call(
        _outproj_body,
        grid=(B, S // tm),
        in_specs=[
            pl.BlockSpec((1, tm, D), lambda b, t: (b, t, 0)),
            pl.BlockSpec((1, N_PAIRS_A, tm, LANES), lambda b, t: (b, 0, t, 0)),
            pl.BlockSpec((1, tm, B_WIDTH), lambda b, t: (b, t, 0)),
            pl.BlockSpec(w.shape, lambda b, t: (0, 0)),
        ],
        out_specs=pl.BlockSpec((1, tm, D), lambda b, t: (b, t, 0)),
        out_shape=jax.ShapeDtypeStruct((B, S, D), F32),
        compiler_params=pltpu.CompilerParams(
            dimension_semantics=("arbitrary", "arbitrary"), vmem_limit_bytes=VMEM_LIMIT),
        name="out_proj",
    )(x, oa, ob, w)


def _ffn_body(x_ref, xp_ref, xn_ref, g_ref, wg_ref, wv_ref, cwg_ref, cwv_ref, cbg_ref, cbv_ref,
              wd_ref, fn_ref, o_ref, h_scr, acc_scr, *, final):
    t = pl.program_id(1)
    c = pl.program_id(2)
    nt = pl.num_programs(1)
    ncol = pl.num_programs(2)
    tm = x_ref.shape[1]

    @pl.when(c == 0)
    def _():
        h_scr[HALO:HALO + tm] = _rms(x_ref[0], g_ref[...]).astype(BF16)
        hp = _rms(xp_ref[0], g_ref[...]).astype(BF16)
        h_scr[0:HALO] = jnp.where(t == 0, jnp.zeros_like(hp), hp)
        hn = _rms(xn_ref[0], g_ref[...]).astype(BF16)
        h_scr[HALO + tm:2 * HALO + tm] = jnp.where(t == nt - 1, jnp.zeros_like(hn), hn)

    h = h_scr[...]
    rows = tm + 2 * HALO

    def conv(w_ref, cw_ref, cb_ref):
        up = jnp.dot(h, w_ref[...], preferred_element_type=F32)
        prev = pltpu.roll(up, 1, 0)[HALO:HALO + tm]
        nxt = pltpu.roll(up, rows - 1, 0)[HALO:HALO + tm]
        mid = up[HALO:HALO + tm]
        return cb_ref[...] + prev * cw_ref[0:1, :] + mid * cw_ref[1:2, :] + nxt * cw_ref[2:3, :]

    ug = conv(wg_ref, cwg_ref, cbg_ref)
    uv = conv(wv_ref, cwv_ref, cbv_ref)
    act = (ug * (1.0 / (1.0 + jnp.exp(-ug))) * uv).astype(BF16)
    part = jnp.dot(act, wd_ref[...], preferred_element_type=F32)

    @pl.when(c == 0)
    def _():
        acc_scr[...] = x_ref[0] + part

    @pl.when(c > 0)
    def _():
        acc_scr[...] += part

    @pl.when(c == ncol - 1)
    def _():
        y = acc_scr[...]
        o_ref[0] = _rms(y, fn_ref[...]) if final else y


def _ffn(x, g, w_up, conv_w, conv_b, w_down, fn, final):
    B, S, D = x.shape
    tm = TM_FFN
    dff = w_down.shape[0]
    ncol = 2
    cw = dff // ncol
    assert cw % LANES == 0 and S % tm == 0 and tm % HALO == 0
    hb = tm // HALO
    nhb = S // HALO
    return pl.pallas_call(
        functools.partial(_ffn_body, final=final),
        grid=(B, S // tm, ncol),
        in_specs=[
            pl.BlockSpec((1, tm, D), lambda b, t, c: (b, t, 0)),
            pl.BlockSpec((1, HALO, D), lambda b, t, c: (b, jnp.maximum(t * hb - 1, 0), 0)),
            pl.BlockSpec((1, HALO, D), lambda b, t, c: (b, jnp.minimum((t + 1) * hb, nhb - 1), 0)),
            pl.BlockSpec((1, D), lambda b, t, c: (0, 0)),
            pl.BlockSpec((D, cw), lambda b, t, c: (0, c)),
            pl.BlockSpec((D, cw), lambda b, t, c: (0, ncol + c)),
            pl.BlockSpec((CONV_WIDTH, cw), lambda b, t, c: (0, c)),
            pl.BlockSpec((CONV_WIDTH, cw), lambda b, t, c: (0, ncol + c)),
            pl.BlockSpec((1, cw), lambda b, t, c: (0, c)),
            pl.BlockSpec((1, cw), lambda b, t, c: (0, ncol + c)),
            pl.BlockSpec((cw, D), lambda b, t, c: (c, 0)),
            pl.BlockSpec((1, D), lambda b, t, c: (0, 0)),
        ],
        out_specs=pl.BlockSpec((1, tm, D), lambda b, t, c: (b, t, 0)),
        out_shape=jax.ShapeDtypeStruct((B, S, D), F32),
        scratch_shapes=[pltpu.VMEM((tm + 2 * HALO, D), BF16), pltpu.VMEM((tm, D), F32)],
        compiler_params=pltpu.CompilerParams(
            dimension_semantics=("arbitrary", "arbitrary", "arbitrary"), vmem_limit_bytes=VMEM_LIMIT),
        name="ffn",
    )(x, x, x, g, w_up, w_up, conv_w, conv_w, conv_b, conv_b, w_down, fn)


def _rope_tables(S):
    rows = S // GRID_W
    row = jnp.repeat(jnp.arange(rows), GRID_W).astype(F32)
    col = jnp.tile(jnp.arange(GRID_W), rows).astype(F32)
    n = HEAD_DIM // 4
    inv = ROPE_THETA ** (-jnp.arange(n, dtype=F32) / n)
    ang = jnp.concatenate([row[:, None] * inv, col[:, None] * inv], axis=-1)
    lane = jnp.arange(LANES)
    pair = (lane % HEAD_DIM) // 2
    sign = jnp.where(lane % 2 == 0, -1.0, 1.0).astype(F32)
    return jnp.cos(ang)[:, pair], jnp.sin(ang)[:, pair] * sign


def _qa_column_order():
    c = jnp.arange(A_WIDTH)
    tile, half, dim = c // LANES, (c % LANES) // HEAD_DIM, c % HEAD_DIM
    return (tile + (A_HEADS // A_KV_HEADS) * half) * HEAD_DIM + dim


def _trunk(x, p):
    depth = p["w_in"].shape[0]
    for l in range(depth):
        qa, ka, vat, qkvb = _in_proj(x, p["attn_norm"][l], p["w_in"][l], p["gq"][l], p["gk"][l], p["cos"], p["sin"])
        oa = _mix_a(qa, ka, vat)
        ob = _mix_b(qkvb, p["bias"])
        x = _out_proj(x, oa, ob, p["w_out"][l])
        x = _ffn(x, p["ffn_norm"][l], p["w_up"][l], p["conv_w"][l], p["conv_b"][l], p["w_down"][l],
                 p["final_norm"], final=(l == depth - 1))
    return x


def kernel(x_prompt, x_sample, attn_norm, w_in, q_norm, k_norm, rel_bias, w_out, ffn_norm, w_up, conv_w, conv_b, w_down, final_norm):
    depth, d_model, _ = w_in.shape
    order = _qa_column_order()
    w_in_p = jnp.concatenate([w_in[:, :, order], w_in[:, :, A_WIDTH:]], axis=-1).astype(BF16)
    w_out_p = jnp.concatenate([w_out[:, order, :], w_out[:, A_WIDTH:, :]], axis=1).astype(BF16)
    params = {
        "attn_norm": attn_norm.reshape(depth, 1, d_model),
        "w_in": w_in_p,
        "gq": (jnp.tile(q_norm, (1, 2)) * Q_SCALE).reshape(depth, 1, LANES),
        "gk": jnp.tile(k_norm, (1, 2)).reshape(depth, 1, LANES),
        "w_out": w_out_p,
        "ffn_norm": ffn_norm.reshape(depth, 1, d_model),
        "w_up": w_up.astype(BF16),
        "conv_w": conv_w,
        "conv_b": conv_b.reshape(depth, 1, -1),
        "w_down": w_down.astype(BF16),
        "final_norm": final_norm.reshape(1, d_model),
        "bias": _bias_tiles(rel_bias),
    }
    outs = []
    for x in (x_prompt, x_sample):
        cos2, sin2 = _rope_tables(x.shape[1])
        outs.append(_trunk(x, dict(params, cos=cos2, sin=sin2)))
    return tuple(outs)
```

```python
import functools
import math

import jax
import jax.numpy as jnp
from jax import lax
from jax.experimental import pallas as pl
from jax.experimental.pallas import tpu as pltpu

F32 = jnp.float32
BF16 = jnp.bfloat16

HEAD_DIM = 64
A_HEADS = 8
A_KV_HEADS = 2
B_HEADS = 8
GRID_W = 64
ROPE_THETA = 10000.0
DILATED_BRANCHES = ((128, 1), (512, 4), (2048, 16))
N_BUCKETS = 32
REL_MAX_DIST = 1024
CONV_WIDTH = 3
EPS = 1e-6
NEG_INF = -1e30
ATTN_SCALE = HEAD_DIM ** -0.5
LOG2E = math.log2(math.e)
Q_SCALE = ATTN_SCALE * LOG2E

LANES = 128
A_WIDTH = A_HEADS * HEAD_DIM
AKV_WIDTH = A_KV_HEADS * HEAD_DIM
B_WIDTH = B_HEADS * HEAD_DIM
N_PAIRS_A = A_WIDTH // LANES
N_PAIRS_B = B_WIDTH // LANES

DILATIONS = tuple(d for _, d in DILATED_BRANCHES)
HALF = DILATED_BRANCHES[0][0] // (2 * DILATED_BRANCHES[0][1])
assert all(w // (2 * d) == HALF for w, d in DILATED_BRANCHES)
QC = 128
KW = QC + 2 * HALF
N_VARIANTS = 3

TM_PROJ = 512
TQ_A = 256
TM_FFN = 512
HALO = 16
ONES_ROWS = 16
VMEM_LIMIT = 56 * 1024 * 1024


def _rms(x, g):
    ms = jnp.mean(x * x, axis=-1, keepdims=True)
    return x * lax.rsqrt(ms + EPS) * g


def _bias_body(tab_ref, out_ref):
    br = pl.program_id(0)
    var = pl.program_id(1)
    dil = jnp.where(br == 0, DILATIONS[0], jnp.where(br == 1, DILATIONS[1], DILATIONS[2]))
    i = lax.broadcasted_iota(jnp.int32, (QC, KW), 0)
    jj = lax.broadcasted_iota(jnp.int32, (QC, KW), 1)
    delta = jj - var * HALF - i
    rel = delta * dil
    nb = N_BUCKETS // 2
    max_exact = nb // 2
    n = jnp.abs(rel)
    large = max_exact + (jnp.log(jnp.maximum(n, 1).astype(F32) / max_exact)
                         / math.log(REL_MAX_DIST / max_exact) * (nb - max_exact)).astype(jnp.int32)
    large = jnp.minimum(large, nb - 1)
    bucket = jnp.where(rel > 0, nb, 0) + jnp.where(n < max_exact, n, large)
    band = jnp.abs(delta) <= HALF
    for h in range(B_HEADS):
        acc = jnp.zeros((QC, KW), F32)
        for b in range(N_BUCKETS):
            acc = jnp.where(bucket == b, tab_ref[b, h], acc)
        out_ref[0, 0, h] = jnp.where(band, acc * LOG2E, NEG_INF)


def _bias_tiles(rel_bias):
    nbr = len(DILATIONS)
    return pl.pallas_call(
        _bias_body,
        grid=(nbr, N_VARIANTS),
        in_specs=[pl.BlockSpec(memory_space=pltpu.SMEM)],
        out_specs=pl.BlockSpec((1, 1, B_HEADS, QC, KW), lambda b, v: (b, v, 0, 0, 0)),
        out_shape=jax.ShapeDtypeStruct((nbr, N_VARIANTS, B_HEADS, QC, KW), F32),
        name="bias_tiles",
    )(rel_bias)


def _inproj_body(x_ref, g_ref, w_ref, gq_ref, gk_ref, cos_ref, sin_ref,
                 qa_ref, ka_ref, vat_ref, qkvb_ref):
    h = _rms(x_ref[0], g_ref[...]).astype(BF16)
    lane = lax.broadcasted_iota(jnp.int32, (1, LANES), 1)
    lo = lane < HEAD_DIM
    even = (lane & 1) == 0
    cos = cos_ref[...]
    sin = sin_ref[...]

    def norm_rope(blk, gain):
        sq = blk * blk
        s_lo = jnp.sum(jnp.where(lo, sq, 0.0), axis=-1, keepdims=True)
        s_hi = jnp.sum(jnp.where(lo, 0.0, sq), axis=-1, keepdims=True)
        r = jnp.where(lo, lax.rsqrt(s_lo * (1.0 / HEAD_DIM) + EPS), lax.rsqrt(s_hi * (1.0 / HEAD_DIM) + EPS))
        y = blk * r * gain
        partner = jnp.where(even, pltpu.roll(y, LANES - 1, 1), pltpu.roll(y, 1, 1))
        return y * cos + partner * sin

    na = A_WIDTH + 2 * AKV_WIDTH
    pa = jnp.dot(h, w_ref[:, 0:na], preferred_element_type=F32)
    for i in range(N_PAIRS_A):
        qa_ref[0, i] = norm_rope(pa[:, LANES * i:LANES * (i + 1)], gq_ref[...]).astype(BF16)
    ka_ref[0] = norm_rope(pa[:, A_WIDTH:A_WIDTH + AKV_WIDTH], gk_ref[...]).astype(BF16)
    vat_ref[0, 0, 0:AKV_WIDTH] = pa[:, A_WIDTH + AKV_WIDTH:na].T.astype(BF16)
    vat_ref[0, 0, AKV_WIDTH:AKV_WIDTH + ONES_ROWS] = jnp.ones((ONES_ROWS, x_ref.shape[1]), BF16)
    pb = jnp.dot(h, w_ref[:, na:na + 3 * B_WIDTH], preferred_element_type=F32)
    qkvb_ref[0, :, 0:B_WIDTH] = pb[:, 0:B_WIDTH] * Q_SCALE
    qkvb_ref[0, :, B_WIDTH:3 * B_WIDTH] = pb[:, B_WIDTH:3 * B_WIDTH]


def _in_proj(x, g, w, gq, gk, cos2, sin2):
    B, S, D = x.shape
    tm = TM_PROJ
    nw = w.shape[1]
    return pl.pallas_call(
        _inproj_body,
        grid=(B, S // tm),
        in_specs=[
            pl.BlockSpec((1, tm, D), lambda b, t: (b, t, 0)),
            pl.BlockSpec((1, D), lambda b, t: (0, 0)),
            pl.BlockSpec((D, nw), lambda b, t: (0, 0)),
            pl.BlockSpec((1, LANES), lambda b, t: (0, 0)),
            pl.BlockSpec((1, LANES), lambda b, t: (0, 0)),
            pl.BlockSpec((tm, LANES), lambda b, t: (t, 0)),
            pl.BlockSpec((tm, LANES), lambda b, t: (t, 0)),
        ],
        out_specs=[
            pl.BlockSpec((1, N_PAIRS_A, tm, LANES), lambda b, t: (b, 0, t, 0)),
            pl.BlockSpec((1, tm, AKV_WIDTH), lambda b, t: (b, t, 0)),
            pl.BlockSpec((1, 1, AKV_WIDTH + ONES_ROWS, tm), lambda b, t: (b, t, 0, 0)),
            pl.BlockSpec((1, tm, 3 * B_WIDTH), lambda b, t: (b, t, 0)),
        ],
        out_shape=[
            jax.ShapeDtypeStruct((B, N_PAIRS_A, S, LANES), BF16),
            jax.ShapeDtypeStruct((B, S, AKV_WIDTH), BF16),
            jax.ShapeDtypeStruct((B, S // tm, AKV_WIDTH + ONES_ROWS, tm), BF16),
            jax.ShapeDtypeStruct((B, S, 3 * B_WIDTH), F32),
        ],
        compiler_params=pltpu.CompilerParams(
            dimension_semantics=("arbitrary", "arbitrary"), vmem_limit_bytes=VMEM_LIMIT),
        name="in_proj",
    )(x, g, w, gq, gk, cos2, sin2)


def _mixa_body(q_ref, k_ref, vt_ref, o_ref, st_a, st_b, acc_scr, q2_scr):
    S = k_ref.shape[1]
    tq = TQ_A
    kc = vt_ref.shape[3]
    n_units = (S // tq) * N_PAIRS_A
    pair_shift = N_PAIRS_A.bit_length() - 1
    lane = lax.broadcasted_iota(jnp.int32, (1, LANES), 1)
    lo = lane < HEAD_DIM
    st_b[...] = jnp.zeros(st_b.shape, F32)

    def stage(s, m_prev, st_cur, st_prv):
        u = jnp.minimum(s, n_units - 1)
        t = lax.shift_right_logical(u, pair_shift)
        i = u & (N_PAIRS_A - 1)
        qb = q_ref[0, i, pl.ds(pl.multiple_of(t * tq, tq), tq), :]
        zero = jnp.zeros_like(qb)
        q2_scr[...] = jnp.concatenate([jnp.where(lo, qb, zero), jnp.where(lo, zero, qb)], axis=0)
        acc_scr[...] = jnp.zeros(acc_scr.shape, F32)

        def chunk(c, mx):
            rows = pl.ds(pl.multiple_of(c * kc, kc), kc)
            e = jnp.exp2(st_prv[rows, :] - m_prev).astype(BF16)
            sn = lax.dot_general(k_ref[0, rows, :], q2_scr[...], (((1,), (1,)), ((), ())),
                                 preferred_element_type=F32)
            st_cur[rows, :] = sn
            acc_scr[...] += jnp.dot(vt_ref[0, c], e, preferred_element_type=F32)
            return jnp.maximum(mx, jnp.max(sn, axis=0, keepdims=True))

        m_cur = lax.fori_loop(0, S // kc, chunk, jnp.full((1, 2 * tq), NEG_INF, F32), unroll=8)

        @pl.when(jnp.logical_and(s >= 1, s <= n_units))
        def _():
            up = s - 1
            tp = lax.shift_right_logical(up, pair_shift)
            ip = up & (N_PAIRS_A - 1)
            ot = acc_scr[...]
            ot = ot[0:AKV_WIDTH] / ot[AKV_WIDTH:AKV_WIDTH + 1]
            oblk = jnp.concatenate([ot[0:HEAD_DIM, 0:tq], ot[HEAD_DIM:2 * HEAD_DIM, tq:2 * tq]], axis=0)
            o_ref[0, ip, pl.ds(pl.multiple_of(tp * tq, tq), tq), :] = oblk.T.astype(BF16)

        return m_cur

    def two_stages(j, m):
        m = stage(2 * j, m, st_a, st_b)
        return stage(2 * j + 1, m, st_b, st_a)

    lax.fori_loop(0, (n_units + 2) // 2, two_stages, jnp.zeros((1, 2 * tq), F32))


def _mix_a(qa, ka, vat):
    B, _, S, _ = qa.shape
    tq = TQ_A
    _, nkc, vrows, kc = vat.shape
    return pl.pallas_call(
        _mixa_body,
        grid=(B,),
        in_specs=[
            pl.BlockSpec((1, N_PAIRS_A, S, LANES), lambda b: (b, 0, 0, 0)),
            pl.BlockSpec((1, S, AKV_WIDTH), lambda b: (b, 0, 0)),
            pl.BlockSpec((1, nkc, vrows, kc), lambda b: (b, 0, 0, 0)),
        ],
        out_specs=pl.BlockSpec((1, N_PAIRS_A, S, LANES), lambda b: (b, 0, 0, 0)),
        out_shape=jax.ShapeDtypeStruct((B, N_PAIRS_A, S, LANES), BF16),
        scratch_shapes=[pltpu.VMEM((S, 2 * tq), F32), pltpu.VMEM((S, 2 * tq), F32), pltpu.VMEM((vrows, 2 * tq), F32),
                        pltpu.VMEM((2 * tq, LANES), BF16)],
        compiler_params=pltpu.CompilerParams(
            dimension_semantics=("arbitrary",), vmem_limit_bytes=VMEM_LIMIT),
        name="mix_a",
    )(qa, ka, vat)


def _mixb_body(q_ref, k_ref, v_ref, bm_ref, o_ref, qd, kd, vd, oacc, macc, dacc):
    S = q_ref.shape[1]
    lane = lax.broadcasted_iota(jnp.int32, (1, LANES), 1)
    lo = lane < HEAD_DIM
    ones = jnp.ones((KW, LANES), BF16)
    rb = 256

    for bi, d in enumerate(DILATIONS):
        L = S // d
        nc = L // QC
        shift = nc.bit_length() - 1
        bshift = (L // rb).bit_length() - 1

        def regroup(blk, carry, d=d, L=L, bshift=bshift):
            r = lax.shift_right_logical(blk, bshift)
            j0 = (blk & (L // rb - 1)) * rb
            src = pl.ds(r + d * j0, rb) if d == 1 else pl.ds(r + d * j0, rb, stride=d)
            dst = pl.ds(pl.multiple_of(blk * rb, rb), rb)
            qd[dst, :] = q_ref[0, src, :].astype(BF16)
            kd[dst, :] = k_ref[0, src, :].astype(BF16)
            vd[dst, :] = v_ref[0, src, :].astype(BF16)
            return carry

        lax.fori_loop(0, S // rb, regroup, 0, unroll=2)

        def body(it, carry, bi=bi, d=d, L=L, nc=nc, shift=shift):
            r = lax.shift_right_logical(it, shift)
            c = it & (nc - 1)
            j0 = c * QC
            ws = jnp.clip(j0 - HALF, 0, L - KW)
            var = jnp.where(c == 0, 0, jnp.where(c == nc - 1, 2, 1))
            base = r * L
            qc = qd[pl.ds(pl.multiple_of(base + j0, QC), QC), :]
            kidx = pl.ds(pl.multiple_of(base + ws, HALF), KW)
            kw = kd[kidx, :]
            vw = jnp.concatenate([vd[kidx, :], ones], axis=1)
            zero = jnp.zeros_like(qc)
            q2 = jnp.concatenate([jnp.where(lo, qc, zero), jnp.where(lo, zero, qc)], axis=0)
            s = lax.dot_general(q2, kw, (((1,), (1,)), ((), ())), preferred_element_type=F32)
            s = s + bm_ref[bi, var].reshape(2 * QC, KW)
            m = jnp.max(s, axis=-1, keepdims=True)
            e = jnp.exp2(s - m).astype(BF16)
            ox = jnp.dot(e, vw, preferred_element_type=F32)
            orow = pl.ds(r + d * j0, QC) if d == 1 else pl.ds(r + d * j0, QC, stride=d)
            oacc[bi, orow, :] = jnp.where(lo, ox[0:QC, 0:LANES], ox[QC:2 * QC, 0:LANES])
            dacc[bi, orow, :] = jnp.where(lo, ox[0:QC, LANES:2 * LANES], ox[QC:2 * QC, LANES:2 * LANES])
            macc[bi, orow, :] = jnp.where(lo, m[0:QC], m[QC:2 * QC])
            return carry

        lax.fori_loop(0, d * nc, body, 0, unroll=8)

    mc = 256
    nbr = len(DILATIONS)

    def merge(ci, carry):
        rws = pl.ds(pl.multiple_of(ci * mc, mc), mc)
        ms = [macc[bi, rws, :] for bi in range(nbr)]
        mm = functools.reduce(jnp.maximum, ms)
        ws = [jnp.exp2(m - mm) for m in ms]
        num = functools.reduce(lambda a, b: a + b, [w * oacc[bi, rws, :] for bi, w in enumerate(ws)])
        den = functools.reduce(lambda a, b: a + b, [w * dacc[bi, rws, :] for bi, w in enumerate(ws)])
        o_ref[0, rws, :] = (num / den).astype(BF16)
        return carry

    lax.fori_loop(0, S // mc, merge, 0, unroll=2)


def _mix_b(qkvb, bmt):
    B, S, _ = qkvb.shape
    nbr = len(DILATIONS)
    assert all((S // d) % 256 == 0 and ((S // d // 256) & (S // d // 256 - 1)) == 0 for d in DILATIONS)
    return pl.pallas_call(
        _mixb_body,
        grid=(B, N_PAIRS_B),
        in_specs=[
            pl.BlockSpec((1, S, LANES), lambda b, p: (b, 0, p)),
            pl.BlockSpec((1, S, LANES), lambda b, p: (b, 0, N_PAIRS_B + p)),
            pl.BlockSpec((1, S, LANES), lambda b, p: (b, 0, 2 * N_PAIRS_B + p)),
            pl.BlockSpec((nbr, N_VARIANTS, 2, QC, KW), lambda b, p: (0, 0, p, 0, 0)),
        ],
        out_specs=pl.BlockSpec((1, S, LANES), lambda b, p: (b, 0, p)),
        out_shape=jax.ShapeDtypeStruct((B, S, B_WIDTH), BF16),
        scratch_shapes=[pltpu.VMEM((S, LANES), BF16)] * 3 + [pltpu.VMEM((nbr, S, LANES), F32)] * 3,
        compiler_params=pltpu.CompilerParams(
            dimension_semantics=("arbitrary", "arbitrary"), vmem_limit_bytes=VMEM_LIMIT),
        name="mix_b",
    )(qkvb, qkvb, qkvb, bmt)


def _outproj_body(x_ref, oa_ref, ob_ref, w_ref, o_ref):
    o = jnp.concatenate([oa_ref[0, i] for i in range(N_PAIRS_A)] + [ob_ref[0]], axis=-1)
    o_ref[0] = x_ref[0] + jnp.dot(o, w_ref[...], preferred_element_type=F32)


def _out_proj(x, oa, ob, w):
    B, S, D = x.shape
    tm = TM_PROJ
    return pl.pallas_call(
        _outproj_body,
        grid=(B, S // tm),
        in_specs=[
            pl.BlockSpec((1, tm, D), lambda b, t: (b, t, 0)),
            pl.BlockSpec((1, N_PAIRS_A, tm, LANES), lambda b, t: (b, 0, t, 0)),
            pl.BlockSpec((1, tm, B_WIDTH), lambda b, t: (b, t, 0)),
            pl.BlockSpec(w.shape, lambda b, t: (0, 0)),
        ],
        out_specs=pl.BlockSpec((1, tm, D), lambda b, t: (b, t, 0)),
        out_shape=jax.ShapeDtypeStruct((B, S, D), F32),
        compiler_params=pltpu.CompilerParams(
            dimension_semantics=("arbitrary", "arbitrary"), vmem_limit_bytes=VMEM_LIMIT),
        name="out_proj",
    )(x, oa, ob, w)


def _ffn_body(x_ref, xp_ref, xn_ref, g_ref, wg_ref, wv_ref, cwg_ref, cwv_ref, cbg_ref, cbv_ref,
              wd_ref, fn_ref, o_ref, h_scr, acc_scr, *, final):
    t = pl.program_id(1)
    c = pl.program_id(2)
    nt = pl.num_programs(1)
    ncol = pl.num_programs(2)
    tm = x_ref.shape[1]

    @pl.when(c == 0)
    def _():
        h_scr[HALO:HALO + tm] = _rms(x_ref[0], g_ref[...]).astype(BF16)
        hp = _rms(xp_ref[0], g_ref[...]).astype(BF16)
        h_scr[0:HALO] = jnp.where(t == 0, jnp.zeros_like(hp), hp)
        hn = _rms(xn_ref[0], g_ref[...]).astype(BF16)
        h_scr[HALO + tm:2 * HALO + tm] = jnp.where(t == nt - 1, jnp.zeros_like(hn), hn)

    h = h_scr[...]
    rows = tm + 2 * HALO

    def conv(w_ref, cw_ref, cb_ref):
        up = jnp.dot(h, w_ref[...], preferred_element_type=F32)
        prev = pltpu.roll(up, 1, 0)[HALO:HALO + tm]
        nxt = pltpu.roll(up, rows - 1, 0)[HALO:HALO + tm]
        mid = up[HALO:HALO + tm]
        return cb_ref[...] + prev * cw_ref[0:1, :] + mid * cw_ref[1:2, :] + nxt * cw_ref[2:3, :]

    ug = conv(wg_ref, cwg_ref, cbg_ref)
    uv = conv(wv_ref, cwv_ref, cbv_ref)
    act = (ug * (1.0 / (1.0 + jnp.exp(-ug))) * uv).astype(BF16)
    part = jnp.dot(act, wd_ref[...], preferred_element_type=F32)

    @pl.when(c == 0)
    def _():
        acc_scr[...] = x_ref[0] + part

    @pl.when(c > 0)
    def _():
        acc_scr[...] += part

    @pl.when(c == ncol - 1)
    def _():
        y = acc_scr[...]
        o_ref[0] = _rms(y, fn_ref[...]) if final else y


def _ffn(x, g, w_up, conv_w, conv_b, w_down, fn, final):
    B, S, D = x.shape
    tm = TM_FFN
    dff = w_down.shape[0]
    ncol = 2
    cw = dff // ncol
    assert cw % LANES == 0 and S % tm == 0 and tm % HALO == 0
    hb = tm // HALO
    nhb = S // HALO
    return pl.pallas_call(
        functools.partial(_ffn_body, final=final),
        grid=(B, S // tm, ncol),
        in_specs=[
            pl.BlockSpec((1, tm, D), lambda b, t, c: (b, t, 0)),
            pl.BlockSpec((1, HALO, D), lambda b, t, c: (b, jnp.maximum(t * hb - 1, 0), 0)),
            pl.BlockSpec((1, HALO, D), lambda b, t, c: (b, jnp.minimum((t + 1) * hb, nhb - 1), 0)),
            pl.BlockSpec((1, D), lambda b, t, c: (0, 0)),
            pl.BlockSpec((D, cw), lambda b, t, c: (0, c)),
            pl.BlockSpec((D, cw), lambda b, t, c: (0, ncol + c)),
            pl.BlockSpec((CONV_WIDTH, cw), lambda b, t, c: (0, c)),
            pl.BlockSpec((CONV_WIDTH, cw), lambda b, t, c: (0, ncol + c)),
            pl.BlockSpec((1, cw), lambda b, t, c: (0, c)),
            pl.BlockSpec((1, cw), lambda b, t, c: (0, ncol + c)),
            pl.BlockSpec((cw, D), lambda b, t, c: (c, 0)),
            pl.BlockSpec((1, D), lambda b, t, c: (0, 0)),
        ],
        out_specs=pl.BlockSpec((1, tm, D), lambda b, t, c: (b, t, 0)),
        out_shape=jax.ShapeDtypeStruct((B, S, D), F32),
        scratch_shapes=[pltpu.VMEM((tm + 2 * HALO, D), BF16), pltpu.VMEM((tm, D), F32)],
        compiler_params=pltpu.CompilerParams(
            dimension_semantics=("arbitrary", "arbitrary", "arbitrary"), vmem_limit_bytes=VMEM_LIMIT),
        name="ffn",
    )(x, x, x, g, w_up, w_up, conv_w, conv_w, conv_b, conv_b, w_down, fn)


def _rope_tables(S):
    rows = S // GRID_W
    row = jnp.repeat(jnp.arange(rows), GRID_W).astype(F32)
    col = jnp.tile(jnp.arange(GRID_W), rows).astype(F32)
    n = HEAD_DIM // 4
    inv = ROPE_THETA ** (-jnp.arange(n, dtype=F32) / n)
    ang = jnp.concatenate([row[:, None] * inv, col[:, None] * inv], axis=-1)
    lane = jnp.arange(LANES)
    pair = (lane % HEAD_DIM) // 2
    sign = jnp.where(lane % 2 == 0, -1.0, 1.0).astype(F32)
    return jnp.cos(ang)[:, pair], jnp.sin(ang)[:, pair] * sign


def _qa_column_order():
    c = jnp.arange(A_WIDTH)
    tile, half, dim = c // LANES, (c % LANES) // HEAD_DIM, c % HEAD_DIM
    return (tile + (A_HEADS // A_KV_HEADS) * half) * HEAD_DIM + dim


def _trunk(x, p):
    depth = p["w_in"].shape[0]
    for l in range(depth):
        qa, ka, vat, qkvb = _in_proj(x, p["attn_norm"][l], p["w_in"][l], p["gq"][l], p["gk"][l], p["cos"], p["sin"])
        oa = _mix_a(qa, ka, vat)
        ob = _mix_b(qkvb, p["bias"])
        x = _out_proj(x, oa, ob, p["w_out"][l])
        x = _ffn(x, p["ffn_norm"][l], p["w_up"][l], p["conv_w"][l], p["conv_b"][l], p["w_down"][l],
                 p["final_norm"], final=(l == depth - 1))
    return x


def kernel(x_prompt, x_sample, attn_norm, w_in, q_norm, k_norm, rel_bias, w_out, ffn_norm, w_up, conv_w, conv_b, w_down, final_norm):
    depth, d_model, _ = w_in.shape
    order = _qa_column_order()
    w_in_p = jnp.concatenate([w_in[:, :, order], w_in[:, :, A_WIDTH:]], axis=-1).astype(BF16)
    w_out_p = jnp.concatenate([w_out[:, order, :], w_out[:, A_WIDTH:, :]], axis=1).astype(BF16)
    params = {
        "attn_norm": attn_norm.reshape(depth, 1, d_model),
        "w_in": w_in_p,
        "gq": (jnp.tile(q_norm, (1, 2)) * Q_SCALE).reshape(depth, 1, LANES),
        "gk": jnp.tile(k_norm, (1, 2)).reshape(depth, 1, LANES),
        "w_out": w_out_p,
        "ffn_norm": ffn_norm.reshape(depth, 1, d_model),
        "w_up": w_up.astype(BF16),
        "conv_w": conv_w,
        "conv_b": conv_b.reshape(depth, 1, -1),
        "w_down": w_down.astype(BF16),
        "final_norm": final_norm.reshape(1, d_model),
        "bias": _bias_tiles(rel_bias),
    }
    outs = []
    for x in (x_prompt, x_sample):
        cos2, sin2 = _rope_tables(x.shape[1])
        outs.append(_trunk(x, dict(params, cos=cos2, sin=sin2)))
    return tuple(outs)
```

```python
import functools
import math

import jax
import jax.numpy as jnp
from jax import lax
from jax.experimental import pallas as pl
from jax.experimental.pallas import tpu as pltpu

F32 = jnp.float32
BF16 = jnp.bfloat16

HEAD_DIM = 64
A_HEADS = 8
A_KV_HEADS = 2
B_HEADS = 8
GRID_W = 64
ROPE_THETA = 10000.0
DILATED_BRANCHES = ((128, 1), (512, 4), (2048, 16))
N_BUCKETS = 32
REL_MAX_DIST = 1024
CONV_WIDTH = 3
EPS = 1e-6
NEG_INF = -1e30
ATTN_SCALE = HEAD_DIM ** -0.5
LOG2E = math.log2(math.e)
Q_SCALE = ATTN_SCALE * LOG2E

LANES = 128
A_WIDTH = A_HEADS * HEAD_DIM
AKV_WIDTH = A_KV_HEADS * HEAD_DIM
B_WIDTH = B_HEADS * HEAD_DIM
N_PAIRS_A = A_WIDTH // LANES
N_PAIRS_B = B_WIDTH // LANES

DILATIONS = tuple(d for _, d in DILATED_BRANCHES)
HALF = DILATED_BRANCHES[0][0] // (2 * DILATED_BRANCHES[0][1])
assert all(w // (2 * d) == HALF for w, d in DILATED_BRANCHES)
QC = 128
KW = QC + 2 * HALF
N_VARIANTS = 3

TM_PROJ = 512
TQ_A = 256
TM_FFN = 512
FFN_CHUNK = 512
HALO = 16
ONES_ROWS = 16
VMEM_LIMIT = 56 * 1024 * 1024


def _rms(x, g):
    ms = jnp.mean(x * x, axis=-1, keepdims=True)
    return x * lax.rsqrt(ms + EPS) * g


def _bias_body(tab_ref, out_ref):
    br = pl.program_id(0)
    var = pl.program_id(1)
    dil = jnp.where(br == 0, DILATIONS[0], jnp.where(br == 1, DILATIONS[1], DILATIONS[2]))
    i = lax.broadcasted_iota(jnp.int32, (QC, KW), 0)
    jj = lax.broadcasted_iota(jnp.int32, (QC, KW), 1)
    delta = jj - var * HALF - i
    rel = delta * dil
    nb = N_BUCKETS // 2
    max_exact = nb // 2
    n = jnp.abs(rel)
    large = max_exact + (jnp.log(jnp.maximum(n, 1).astype(F32) / max_exact)
                         / math.log(REL_MAX_DIST / max_exact) * (nb - max_exact)).astype(jnp.int32)
    large = jnp.minimum(large, nb - 1)
    bucket = jnp.where(rel > 0, nb, 0) + jnp.where(n < max_exact, n, large)
    band = jnp.abs(delta) <= HALF
    for h in range(B_HEADS):
        acc = jnp.zeros((QC, KW), F32)
        for b in range(N_BUCKETS):
            acc = jnp.where(bucket == b, tab_ref[b, h], acc)
        out_ref[0, 0, h] = jnp.where(band, acc * LOG2E, NEG_INF)


def _bias_tiles(rel_bias):
    nbr = len(DILATIONS)
    return pl.pallas_call(
        _bias_body,
        grid=(nbr, N_VARIANTS),
        in_specs=[pl.BlockSpec(memory_space=pltpu.SMEM)],
        out_specs=pl.BlockSpec((1, 1, B_HEADS, QC, KW), lambda b, v: (b, v, 0, 0, 0)),
        out_shape=jax.ShapeDtypeStruct((nbr, N_VARIANTS, B_HEADS, QC, KW), F32),
        name="bias_tiles",
    )(rel_bias)


def _inproj_body(x_ref, g_ref, w_ref, gq_ref, gk_ref, cos_ref, sin_ref,
                 qa_ref, ka_ref, vat_ref, qkvb_ref):
    h = _rms(x_ref[0], g_ref[...]).astype(BF16)
    lane = lax.broadcasted_iota(jnp.int32, (1, LANES), 1)
    lo = lane < HEAD_DIM
    even = (lane & 1) == 0
    cos = cos_ref[...]
    sin = sin_ref[...]

    def norm_rope(blk, gain):
        sq = blk * blk
        s_lo = jnp.sum(jnp.where(lo, sq, 0.0), axis=-1, keepdims=True)
        s_hi = jnp.sum(jnp.where(lo, 0.0, sq), axis=-1, keepdims=True)
        r = jnp.where(lo, lax.rsqrt(s_lo * (1.0 / HEAD_DIM) + EPS), lax.rsqrt(s_hi * (1.0 / HEAD_DIM) + EPS))
        y = blk * r * gain
        partner = jnp.where(even, pltpu.roll(y, LANES - 1, 1), pltpu.roll(y, 1, 1))
        return y * cos + partner * sin

    na = A_WIDTH + 2 * AKV_WIDTH
    pa = jnp.dot(h, w_ref[:, 0:na], preferred_element_type=F32)
    for i in range(N_PAIRS_A):
        qa_ref[0, i] = norm_rope(pa[:, LANES * i:LANES * (i + 1)], gq_ref[...]).astype(BF16)
    ka_ref[0] = norm_rope(pa[:, A_WIDTH:A_WIDTH + AKV_WIDTH], gk_ref[...]).astype(BF16)
    vat_ref[0, 0, 0:AKV_WIDTH] = pa[:, A_WIDTH + AKV_WIDTH:na].T.astype(BF16)
    vat_ref[0, 0, AKV_WIDTH:AKV_WIDTH + ONES_ROWS] = jnp.ones((ONES_ROWS, x_ref.shape[1]), BF16)
    pb = jnp.dot(h, w_ref[:, na:na + 3 * B_WIDTH], preferred_element_type=F32)
    qkvb_ref[0, :, 0:B_WIDTH] = pb[:, 0:B_WIDTH] * Q_SCALE
    qkvb_ref[0, :, B_WIDTH:3 * B_WIDTH] = pb[:, B_WIDTH:3 * B_WIDTH]


def _in_proj(x, g, w, gq, gk, cos2, sin2):
    B, S, D = x.shape
    tm = TM_PROJ
    nw = w.shape[1]
    return pl.pallas_call(
        _inproj_body,
        grid=(B, S // tm),
        in_specs=[
            pl.BlockSpec((1, tm, D), lambda b, t: (b, t, 0)),
            pl.BlockSpec((1, D), lambda b, t: (0, 0)),
            pl.BlockSpec((D, nw), lambda b, t: (0, 0)),
            pl.BlockSpec((1, LANES), lambda b, t: (0, 0)),
            pl.BlockSpec((1, LANES), lambda b, t: (0, 0)),
            pl.BlockSpec((tm, LANES), lambda b, t: (t, 0)),
            pl.BlockSpec((tm, LANES), lambda b, t: (t, 0)),
        ],
        out_specs=[
            pl.BlockSpec((1, N_PAIRS_A, tm, LANES), lambda b, t: (b, 0, t, 0)),
            pl.BlockSpec((1, tm, AKV_WIDTH), lambda b, t: (b, t, 0)),
            pl.BlockSpec((1, 1, AKV_WIDTH + ONES_ROWS, tm), lambda b, t: (b, t, 0, 0)),
            pl.BlockSpec((1, tm, 3 * B_WIDTH), lambda b, t: (b, t, 0)),
        ],
        out_shape=[
            jax.ShapeDtypeStruct((B, N_PAIRS_A, S, LANES), BF16),
            jax.ShapeDtypeStruct((B, S, AKV_WIDTH), BF16),
            jax.ShapeDtypeStruct((B, S // tm, AKV_WIDTH + ONES_ROWS, tm), BF16),
            jax.ShapeDtypeStruct((B, S, 3 * B_WIDTH), F32),
        ],
        compiler_params=pltpu.CompilerParams(
            dimension_semantics=("arbitrary", "arbitrary"), vmem_limit_bytes=VMEM_LIMIT),
        name="in_proj",
    )(x, g, w, gq, gk, cos2, sin2)


def _mixa_body(q_ref, k_ref, vt_ref, o_ref, st_a, st_b, acc_scr, q2_scr):
    S = k_ref.shape[1]
    tq = TQ_A
    kc = vt_ref.shape[3]
    n_units = (S // tq) * N_PAIRS_A
    pair_shift = N_PAIRS_A.bit_length() - 1
    lane = lax.broadcasted_iota(jnp.int32, (1, LANES), 1)
    lo = lane < HEAD_DIM
    st_b[...] = jnp.zeros(st_b.shape, F32)

    def stage(s, m_prev, st_cur, st_prv):
        u = jnp.minimum(s, n_units - 1)
        t = lax.shift_right_logical(u, pair_shift)
        i = u & (N_PAIRS_A - 1)
        qb = q_ref[0, i, pl.ds(pl.multiple_of(t * tq, tq), tq), :]
        zero = jnp.zeros_like(qb)
        q2_scr[...] = jnp.concatenate([jnp.where(lo, qb, zero), jnp.where(lo, zero, qb)], axis=0)
        acc_scr[...] = jnp.zeros(acc_scr.shape, F32)

        def chunk(c, mx):
            rows = pl.ds(pl.multiple_of(c * kc, kc), kc)
            e = jnp.exp2(st_prv[rows, :] - m_prev).astype(BF16)
            sn = lax.dot_general(k_ref[0, rows, :], q2_scr[...], (((1,), (1,)), ((), ())),
                                 preferred_element_type=F32)
            st_cur[rows, :] = sn
            acc_scr[...] += jnp.dot(vt_ref[0, c], e, preferred_element_type=F32)
            return jnp.maximum(mx, jnp.max(sn, axis=0, keepdims=True))

        m_cur = lax.fori_loop(0, S // kc, chunk, jnp.full((1, 2 * tq), NEG_INF, F32), unroll=8)

        @pl.when(jnp.logical_and(s >= 1, s <= n_units))
        def _():
            up = s - 1
            tp = lax.shift_right_logical(up, pair_shift)
            ip = up & (N_PAIRS_A - 1)
            ot = acc_scr[...]
            ot = ot[0:AKV_WIDTH] / ot[AKV_WIDTH:AKV_WIDTH + 1]
            oblk = jnp.concatenate([ot[0:HEAD_DIM, 0:tq], ot[HEAD_DIM:2 * HEAD_DIM, tq:2 * tq]], axis=0)
            o_ref[0, ip, pl.ds(pl.multiple_of(tp * tq, tq), tq), :] = oblk.T.astype(BF16)

        return m_cur

    def two_stages(j, m):
        m = stage(2 * j, m, st_a, st_b)
        return stage(2 * j + 1, m, st_b, st_a)

    lax.fori_loop(0, (n_units + 2) // 2, two_stages, jnp.zeros((1, 2 * tq), F32))


def _mix_a(qa, ka, vat):
    B, _, S, _ = qa.shape
    tq = TQ_A
    _, nkc, vrows, kc = vat.shape
    return pl.pallas_call(
        _mixa_body,
        grid=(B,),
        in_specs=[
            pl.BlockSpec((1, N_PAIRS_A, S, LANES), lambda b: (b, 0, 0, 0)),
            pl.BlockSpec((1, S, AKV_WIDTH), lambda b: (b, 0, 0)),
            pl.BlockSpec((1, nkc, vrows, kc), lambda b: (b, 0, 0, 0)),
        ],
        out_specs=pl.BlockSpec((1, N_PAIRS_A, S, LANES), lambda b: (b, 0, 0, 0)),
        out_shape=jax.ShapeDtypeStruct((B, N_PAIRS_A, S, LANES), BF16),
        scratch_shapes=[pltpu.VMEM((S, 2 * tq), F32), pltpu.VMEM((S, 2 * tq), F32), pltpu.VMEM((vrows, 2 * tq), F32),
                        pltpu.VMEM((2 * tq, LANES), BF16)],
        compiler_params=pltpu.CompilerParams(
            dimension_semantics=("arbitrary",), vmem_limit_bytes=VMEM_LIMIT),
        name="mix_a",
    )(qa, ka, vat)


def _mixb_body(q_ref, k_ref, v_ref, bm_ref, o_ref, qd, kd, vd, oacc, macc, dacc):
    S = q_ref.shape[1]
    lane = lax.broadcasted_iota(jnp.int32, (1, LANES), 1)
    lo = lane < HEAD_DIM
    ones = jnp.ones((KW, LANES), BF16)
    rb = 256

    for bi, d in enumerate(DILATIONS):
        L = S // d
        nc = L // QC
        shift = nc.bit_length() - 1
        bshift = (L // rb).bit_length() - 1

        def regroup(blk, carry, d=d, L=L, bshift=bshift):
            r = lax.shift_right_logical(blk, bshift)
            j0 = (blk & (L // rb - 1)) * rb
            src = pl.ds(r + d * j0, rb) if d == 1 else pl.ds(r + d * j0, rb, stride=d)
            dst = pl.ds(pl.multiple_of(blk * rb, rb), rb)
            qd[dst, :] = q_ref[0, src, :].astype(BF16)
            kd[dst, :] = k_ref[0, src, :].astype(BF16)
            vd[dst, :] = v_ref[0, src, :].astype(BF16)
            return carry

        lax.fori_loop(0, S // rb, regroup, 0, unroll=2)

        def body(it, carry, bi=bi, d=d, L=L, nc=nc, shift=shift):
            r = lax.shift_right_logical(it, shift)
            c = it & (nc - 1)
            j0 = c * QC
            ws = jnp.clip(j0 - HALF, 0, L - KW)
            var = jnp.where(c == 0, 0, jnp.where(c == nc - 1, 2, 1))
            base = r * L
            qc = qd[pl.ds(pl.multiple_of(base + j0, QC), QC), :]
            kidx = pl.ds(pl.multiple_of(base + ws, HALF), KW)
            kw = kd[kidx, :]
            vw = jnp.concatenate([vd[kidx, :], ones], axis=1)
            zero = jnp.zeros_like(qc)
            q2 = jnp.concatenate([jnp.where(lo, qc, zero), jnp.where(lo, zero, qc)], axis=0)
            s = lax.dot_general(q2, kw, (((1,), (1,)), ((), ())), preferred_element_type=F32)
            s = s + bm_ref[bi, var].reshape(2 * QC, KW)
            m = jnp.max(s, axis=-1, keepdims=True)
            e = jnp.exp2(s - m).astype(BF16)
            ox = jnp.dot(e, vw, preferred_element_type=F32)
            orow = pl.ds(r + d * j0, QC) if d == 1 else pl.ds(r + d * j0, QC, stride=d)
            oacc[bi, orow, :] = jnp.where(lo, ox[0:QC, 0:LANES], ox[QC:2 * QC, 0:LANES])
            dacc[bi, orow, :] = jnp.where(lo, ox[0:QC, LANES:2 * LANES], ox[QC:2 * QC, LANES:2 * LANES])
            macc[bi, orow, :] = jnp.where(lo, m[0:QC], m[QC:2 * QC])
            return carry

        lax.fori_loop(0, d * nc, body, 0, unroll=8)

    mc = 256
    nbr = len(DILATIONS)

    def merge(ci, carry):
        rws = pl.ds(pl.multiple_of(ci * mc, mc), mc)
        ms = [macc[bi, rws, :] for bi in range(nbr)]
        mm = functools.reduce(jnp.maximum, ms)
        ws = [jnp.exp2(m - mm) for m in ms]
        num = functools.reduce(lambda a, b: a + b, [w * oacc[bi, rws, :] for bi, w in enumerate(ws)])
        den = functools.reduce(lambda a, b: a + b, [w * dacc[bi, rws, :] for bi, w in enumerate(ws)])
        o_ref[0, rws, :] = (num / den).astype(BF16)
        return carry

    lax.fori_loop(0, S // mc, merge, 0, unroll=2)


def _mix_b(qkvb, bmt):
    B, S, _ = qkvb.shape
    nbr = len(DILATIONS)
    assert all((S // d) % 256 == 0 and ((S // d // 256) & (S // d // 256 - 1)) == 0 for d in DILATIONS)
    return pl.pallas_call(
        _mixb_body,
        grid=(B, N_PAIRS_B),
        in_specs=[
            pl.BlockSpec((1, S, LANES), lambda b, p: (b, 0, p)),
            pl.BlockSpec((1, S, LANES), lambda b, p: (b, 0, N_PAIRS_B + p)),
            pl.BlockSpec((1, S, LANES), lambda b, p: (b, 0, 2 * N_PAIRS_B + p)),
            pl.BlockSpec((nbr, N_VARIANTS, 2, QC, KW), lambda b, p: (0, 0, p, 0, 0)),
        ],
        out_specs=pl.BlockSpec((1, S, LANES), lambda b, p: (b, 0, p)),
        out_shape=jax.ShapeDtypeStruct((B, S, B_WIDTH), BF16),
        scratch_shapes=[pltpu.VMEM((S, LANES), BF16)] * 3 + [pltpu.VMEM((nbr, S, LANES), F32)] * 3,
        compiler_params=pltpu.CompilerParams(
            dimension_semantics=("arbitrary", "arbitrary"), vmem_limit_bytes=VMEM_LIMIT),
        name="mix_b",
    )(qkvb, qkvb, qkvb, bmt)


def _ffn_body(x_ref, xp_ref, xn_ref, oa_ref, oap_ref, oan_ref, ob_ref, obp_ref, obn_ref,
              wo_ref, g_ref, wup_ref, cw_ref, cb_ref, wd_ref, fn_ref, o_ref, o_scr, *, final):
    t = pl.program_id(1)
    nt = pl.num_programs(1)
    tm = x_ref.shape[1]
    rows = tm + 2 * HALO
    dff = wd_ref.shape[0]

    for i in range(N_PAIRS_A + 1):
        cols = slice(LANES * i, LANES * (i + 1)) if i < N_PAIRS_A else slice(A_WIDTH, A_WIDTH + B_WIDTH)
        o_scr[0:HALO, cols] = oap_ref[0, i] if i < N_PAIRS_A else obp_ref[0]
        o_scr[HALO:HALO + tm, cols] = oa_ref[0, i] if i < N_PAIRS_A else ob_ref[0]
        o_scr[HALO + tm:rows, cols] = oan_ref[0, i] if i < N_PAIRS_A else obn_ref[0]
    x_ext = jnp.concatenate([xp_ref[0], x_ref[0], xn_ref[0]], axis=0)
    x1 = x_ext + jnp.dot(o_scr[...], wo_ref[...], preferred_element_type=F32)
    rid = lax.broadcasted_iota(jnp.int32, (rows, 1), 0)
    inside = jnp.logical_and(jnp.logical_or(t > 0, rid >= HALO), jnp.logical_or(t < nt - 1, rid < HALO + tm))
    h = jnp.where(inside, _rms(x1, g_ref[...]), 0.0).astype(BF16)

    y = x1[HALO:HALO + tm]
    for c0 in range(0, dff, FFN_CHUNK):
        cw = min(FFN_CHUNK, dff - c0)

        def conv(off):
            up = jnp.dot(h, wup_ref[:, off:off + cw], preferred_element_type=F32)
            prev = pltpu.roll(up, 1, 0)[HALO:HALO + tm]
            nxt = pltpu.roll(up, rows - 1, 0)[HALO:HALO + tm]
            mid = up[HALO:HALO + tm]
            return (cb_ref[:, off:off + cw] + prev * cw_ref[0:1, off:off + cw]
                    + mid * cw_ref[1:2, off:off + cw] + nxt * cw_ref[2:3, off:off + cw])

        ug = conv(c0)
        uv = conv(dff + c0)
        act = (ug * (1.0 / (1.0 + jnp.exp(-ug))) * uv).astype(BF16)
        y = y + jnp.dot(act, wd_ref[c0:c0 + cw, :], preferred_element_type=F32)
    o_ref[0] = _rms(y, fn_ref[...]) if final else y


def _ffn(x, oa, ob, w_out, g, w_up, conv_w, conv_b, w_down, fn, final):
    B, S, D = x.shape
    tm = TM_FFN
    assert S % tm == 0 and tm % HALO == 0 and w_down.shape[0] % LANES == 0 and FFN_CHUNK % LANES == 0
    hb = tm // HALO
    nhb = S // HALO

    def prev_blk(t):
        return jnp.maximum(t * hb - 1, 0)

    def next_blk(t):
        return jnp.minimum((t + 1) * hb, nhb - 1)

    def resident(a):
        return pl.BlockSpec(a.shape, lambda b, t: (0,) * a.ndim, pipeline_mode=pl.Buffered(1))

    return pl.pallas_call(
        functools.partial(_ffn_body, final=final),
        grid=(B, S // tm),
        in_specs=[
            pl.BlockSpec((1, tm, D), lambda b, t: (b, t, 0)),
            pl.BlockSpec((1, HALO, D), lambda b, t: (b, prev_blk(t), 0)),
            pl.BlockSpec((1, HALO, D), lambda b, t: (b, next_blk(t), 0)),
            pl.BlockSpec((1, N_PAIRS_A, tm, LANES), lambda b, t: (b, 0, t, 0)),
            pl.BlockSpec((1, N_PAIRS_A, HALO, LANES), lambda b, t: (b, 0, prev_blk(t), 0)),
            pl.BlockSpec((1, N_PAIRS_A, HALO, LANES), lambda b, t: (b, 0, next_blk(t), 0)),
            pl.BlockSpec((1, tm, B_WIDTH), lambda b, t: (b, t, 0)),
            pl.BlockSpec((1, HALO, B_WIDTH), lambda b, t: (b, prev_blk(t), 0)),
            pl.BlockSpec((1, HALO, B_WIDTH), lambda b, t: (b, next_blk(t), 0)),
            resident(w_out), resident(g), resident(w_up), resident(conv_w), resident(conv_b),
            resident(w_down), resident(fn),
        ],
        out_specs=pl.BlockSpec((1, tm, D), lambda b, t: (b, t, 0)),
        out_shape=jax.ShapeDtypeStruct((B, S, D), F32),
        scratch_shapes=[pltpu.VMEM((tm + 2 * HALO, A_WIDTH + B_WIDTH), BF16)],
        compiler_params=pltpu.CompilerParams(
            dimension_semantics=("arbitrary", "arbitrary"), vmem_limit_bytes=VMEM_LIMIT),
        name="ffn",
    )(x, x, x, oa, oa, oa, ob, ob, ob, w_out, g, w_up, conv_w, conv_b, w_down, fn)


def _rope_tables(S):
    rows = S // GRID_W
    row = jnp.repeat(jnp.arange(rows), GRID_W).astype(F32)
    col = jnp.tile(jnp.arange(GRID_W), rows).astype(F32)
    n = HEAD_DIM // 4
    inv = ROPE_THETA ** (-jnp.arange(n, dtype=F32) / n)
    ang = jnp.concatenate([row[:, None] * inv, col[:, None] * inv], axis=-1)
    lane = jnp.arange(LANES)
    pair = (lane % HEAD_DIM) // 2
    sign = jnp.where(lane % 2 == 0, -1.0, 1.0).astype(F32)
    return jnp.cos(ang)[:, pair], jnp.sin(ang)[:, pair] * sign


def _qa_column_order():
    c = jnp.arange(A_WIDTH)
    tile, half, dim = c // LANES, (c % LANES) // HEAD_DIM, c % HEAD_DIM
    return (tile + (A_HEADS // A_KV_HEADS) * half) * HEAD_DIM + dim


def _trunk(x, p):
    depth = p["w_in"].shape[0]
    for l in range(depth):
        qa, ka, vat, qkvb = _in_proj(x, p["attn_norm"][l], p["w_in"][l], p["gq"][l], p["gk"][l], p["cos"], p["sin"])
        oa = _mix_a(qa, ka, vat)
        ob = _mix_b(qkvb, p["bias"])
        x = _ffn(x, oa, ob, p["w_out"][l], p["ffn_norm"][l], p["w_up"][l], p["conv_w"][l], p["conv_b"][l],
                 p["w_down"][l], p["final_norm"], final=(l == depth - 1))
    return x


def kernel(x_prompt, x_sample, attn_norm, w_in, q_norm, k_norm, rel_bias, w_out, ffn_norm, w_up, conv_w, conv_b, w_down, final_norm):
    depth, d_model, _ = w_in.shape
    order = _qa_column_order()
    w_in_p = jnp.concatenate([w_in[:, :, order], w_in[:, :, A_WIDTH:]], axis=-1).astype(BF16)
    w_out_p = jnp.concatenate([w_out[:, order, :], w_out[:, A_WIDTH:, :]], axis=1).astype(BF16)
    params = {
        "attn_norm": attn_norm.reshape(depth, 1, d_model),
        "w_in": w_in_p,
        "gq": (jnp.tile(q_norm, (1, 2)) * Q_SCALE).reshape(depth, 1, LANES),
        "gk": jnp.tile(k_norm, (1, 2)).reshape(depth, 1, LANES),
        "w_out": w_out_p,
        "ffn_norm": ffn_norm.reshape(depth, 1, d_model),
        "w_up": w_up.astype(BF16),
        "conv_w": conv_w,
        "conv_b": conv_b.reshape(depth, 1, -1),
        "w_down": w_down.astype(BF16),
        "final_norm": final_norm.reshape(1, d_model),
        "bias": _bias_tiles(rel_bias),
    }
    outs = []
    for x in (x_prompt, x_sample):
        cos2, sin2 = _rope_tables(x.shape[1])
        outs.append(_trunk(x, dict(params, cos=cos2, sin=sin2)))
    return tuple(outs)
```

```python
import functools
import math

import jax
import jax.numpy as jnp
from jax import lax
from jax.experimental import pallas as pl
from jax.experimental.pallas import tpu as pltpu

F32 = jnp.float32
BF16 = jnp.bfloat16

HEAD_DIM = 64
A_HEADS = 8
A_KV_HEADS = 2
B_HEADS = 8
GRID_W = 64
ROPE_THETA = 10000.0
DILATED_BRANCHES = ((128, 1), (512, 4), (2048, 16))
N_BUCKETS = 32
REL_MAX_DIST = 1024
CONV_WIDTH = 3
EPS = 1e-6
NEG_INF = -1e30
ATTN_SCALE = HEAD_DIM ** -0.5
LOG2E = math.log2(math.e)
Q_SCALE = ATTN_SCALE * LOG2E

LANES = 128
A_WIDTH = A_HEADS * HEAD_DIM
AKV_WIDTH = A_KV_HEADS * HEAD_DIM
B_WIDTH = B_HEADS * HEAD_DIM
N_PAIRS_A = A_WIDTH // LANES
N_PAIRS_B = B_WIDTH // LANES

DILATIONS = tuple(d for _, d in DILATED_BRANCHES)
HALF = DILATED_BRANCHES[0][0] // (2 * DILATED_BRANCHES[0][1])
assert all(w // (2 * d) == HALF for w, d in DILATED_BRANCHES)
QC = 128
KW = QC + 2 * HALF
N_VARIANTS = 3

TM_PROJ = 512
TQ_A = 256
TM_FFN = 512
FFN_CHUNK = 512
HALO = 16
ONES_ROWS = 16
VMEM_LIMIT = 56 * 1024 * 1024


def _rms(x, g):
    ms = jnp.mean(x * x, axis=-1, keepdims=True)
    return x * lax.rsqrt(ms + EPS) * g


def _bias_body(tab_ref, out_ref):
    br = pl.program_id(0)
    var = pl.program_id(1)
    dil = jnp.where(br == 0, DILATIONS[0], jnp.where(br == 1, DILATIONS[1], DILATIONS[2]))
    i = lax.broadcasted_iota(jnp.int32, (QC, KW), 0)
    jj = lax.broadcasted_iota(jnp.int32, (QC, KW), 1)
    delta = jj - var * HALF - i
    rel = delta * dil
    nb = N_BUCKETS // 2
    max_exact = nb // 2
    n = jnp.abs(rel)
    large = max_exact + (jnp.log(jnp.maximum(n, 1).astype(F32) / max_exact)
                         / math.log(REL_MAX_DIST / max_exact) * (nb - max_exact)).astype(jnp.int32)
    large = jnp.minimum(large, nb - 1)
    bucket = jnp.where(rel > 0, nb, 0) + jnp.where(n < max_exact, n, large)
    band = jnp.abs(delta) <= HALF
    for h in range(B_HEADS):
        acc = jnp.zeros((QC, KW), F32)
        for b in range(N_BUCKETS):
            acc = jnp.where(bucket == b, tab_ref[b, h], acc)
        out_ref[0, 0, h] = jnp.where(band, acc * LOG2E, NEG_INF)


def _bias_tiles(rel_bias):
    nbr = len(DILATIONS)
    return pl.pallas_call(
        _bias_body,
        grid=(nbr, N_VARIANTS),
        in_specs=[pl.BlockSpec(memory_space=pltpu.SMEM)],
        out_specs=pl.BlockSpec((1, 1, B_HEADS, QC, KW), lambda b, v: (b, v, 0, 0, 0)),
        out_shape=jax.ShapeDtypeStruct((nbr, N_VARIANTS, B_HEADS, QC, KW), F32),
        name="bias_tiles",
    )(rel_bias)


def _inproj_body(x_ref, g_ref, w_ref, gq_ref, gk_ref, cos_ref, sin_ref,
                 qa_ref, ka_ref, vat_ref, qb_ref, *rest):
    kv_refs, kv_scr = rest[:-1], rest[-1]
    h = _rms(x_ref[0], g_ref[...]).astype(BF16)
    lane = lax.broadcasted_iota(jnp.int32, (1, LANES), 1)
    lo = lane < HEAD_DIM
    even = (lane & 1) == 0
    cos = cos_ref[...]
    sin = sin_ref[...]

    def norm_rope(blk, gain):
        sq = blk * blk
        s_lo = jnp.sum(jnp.where(lo, sq, 0.0), axis=-1, keepdims=True)
        s_hi = jnp.sum(jnp.where(lo, 0.0, sq), axis=-1, keepdims=True)
        r = jnp.where(lo, lax.rsqrt(s_lo * (1.0 / HEAD_DIM) + EPS), lax.rsqrt(s_hi * (1.0 / HEAD_DIM) + EPS))
        y = blk * r * gain
        partner = jnp.where(even, pltpu.roll(y, LANES - 1, 1), pltpu.roll(y, 1, 1))
        return y * cos + partner * sin

    na = A_WIDTH + 2 * AKV_WIDTH
    pa = jnp.dot(h, w_ref[:, 0:na], preferred_element_type=F32)
    for i in range(N_PAIRS_A):
        qa_ref[0, i] = norm_rope(pa[:, LANES * i:LANES * (i + 1)], gq_ref[...]).astype(BF16)
    ka_ref[0] = norm_rope(pa[:, A_WIDTH:A_WIDTH + AKV_WIDTH], gk_ref[...]).astype(BF16)
    vat_ref[0, 0, 0:AKV_WIDTH] = pa[:, A_WIDTH + AKV_WIDTH:na].T.astype(BF16)
    vat_ref[0, 0, AKV_WIDTH:AKV_WIDTH + ONES_ROWS] = jnp.ones((ONES_ROWS, x_ref.shape[1]), BF16)
    qb_ref[0] = jnp.dot(h, w_ref[:, na:na + B_WIDTH], preferred_element_type=F32) * Q_SCALE
    tm = x_ref.shape[1]
    step = 2 * LANES
    for c0 in range(0, 2 * B_WIDTH, step):
        pkv = jnp.dot(h, w_ref[:, na + B_WIDTH + c0:na + B_WIDTH + c0 + step], preferred_element_type=F32)
        for half in range(step // LANES):
            ct = c0 // LANES + half
            cols = slice(LANES * ct, LANES * (ct + 1))
            tile = pkv[:, LANES * half:LANES * (half + 1)]
            kv_scr[ct] = tile
            for d, kv_ref in zip(DILATIONS, kv_refs):
                if d == 1:
                    kv_ref[0, 0, :, cols] = tile.astype(BF16)
                else:
                    for r in range(d):
                        kv_ref[0, r, :, cols] = kv_scr[ct, pl.ds(r, tm // d, stride=d), :].astype(BF16)


def _in_proj(x, g, w, gq, gk, cos2, sin2):
    B, S, D = x.shape
    tm = TM_PROJ
    nw = w.shape[1]
    return pl.pallas_call(
        _inproj_body,
        grid=(B, S // tm),
        in_specs=[
            pl.BlockSpec((1, tm, D), lambda b, t: (b, t, 0)),
            pl.BlockSpec((1, D), lambda b, t: (0, 0)),
            pl.BlockSpec((D, nw), lambda b, t: (0, 0)),
            pl.BlockSpec((1, LANES), lambda b, t: (0, 0)),
            pl.BlockSpec((1, LANES), lambda b, t: (0, 0)),
            pl.BlockSpec((tm, LANES), lambda b, t: (t, 0)),
            pl.BlockSpec((tm, LANES), lambda b, t: (t, 0)),
        ],
        out_specs=[
            pl.BlockSpec((1, N_PAIRS_A, tm, LANES), lambda b, t: (b, 0, t, 0)),
            pl.BlockSpec((1, tm, AKV_WIDTH), lambda b, t: (b, t, 0)),
            pl.BlockSpec((1, 1, AKV_WIDTH + ONES_ROWS, tm), lambda b, t: (b, t, 0, 0)),
            pl.BlockSpec((1, tm, B_WIDTH), lambda b, t: (b, t, 0)),
        ] + [pl.BlockSpec((1, d, tm // d, 2 * B_WIDTH), lambda b, t: (b, 0, t, 0)) for d in DILATIONS],
        out_shape=[
            jax.ShapeDtypeStruct((B, N_PAIRS_A, S, LANES), BF16),
            jax.ShapeDtypeStruct((B, S, AKV_WIDTH), BF16),
            jax.ShapeDtypeStruct((B, S // tm, AKV_WIDTH + ONES_ROWS, tm), BF16),
            jax.ShapeDtypeStruct((B, S, B_WIDTH), F32),
        ] + [jax.ShapeDtypeStruct((B, d, S // d, 2 * B_WIDTH), BF16) for d in DILATIONS],
        scratch_shapes=[pltpu.VMEM((2 * N_PAIRS_B, tm, LANES), F32)],
        compiler_params=pltpu.CompilerParams(
            dimension_semantics=("arbitrary", "arbitrary"), vmem_limit_bytes=VMEM_LIMIT),
        name="in_proj",
    )(x, g, w, gq, gk, cos2, sin2)


def _mixa_body(q_ref, k_ref, vt_ref, o_ref, st_a, st_b, acc_scr, q2_scr):
    S = k_ref.shape[1]
    tq = TQ_A
    kc = vt_ref.shape[3]
    n_units = (S // tq) * N_PAIRS_A
    pair_shift = N_PAIRS_A.bit_length() - 1
    lane = lax.broadcasted_iota(jnp.int32, (1, LANES), 1)
    lo = lane < HEAD_DIM
    st_b[...] = jnp.zeros(st_b.shape, F32)

    def stage(s, m_prev, st_cur, st_prv):
        u = jnp.minimum(s, n_units - 1)
        t = lax.shift_right_logical(u, pair_shift)
        i = u & (N_PAIRS_A - 1)
        qb = q_ref[0, i, pl.ds(pl.multiple_of(t * tq, tq), tq), :]
        zero = jnp.zeros_like(qb)
        q2_scr[...] = jnp.concatenate([jnp.where(lo, qb, zero), jnp.where(lo, zero, qb)], axis=0)
        acc_scr[...] = jnp.zeros(acc_scr.shape, F32)

        def chunk(c, mx):
            rows = pl.ds(pl.multiple_of(c * kc, kc), kc)
            e = jnp.exp2(st_prv[rows, :] - m_prev).astype(BF16)
            sn = lax.dot_general(k_ref[0, rows, :], q2_scr[...], (((1,), (1,)), ((), ())),
                                 preferred_element_type=F32)
            st_cur[rows, :] = sn
            acc_scr[...] += jnp.dot(vt_ref[0, c], e, preferred_element_type=F32)
            return jnp.maximum(mx, jnp.max(sn, axis=0, keepdims=True))

        m_cur = lax.fori_loop(0, S // kc, chunk, jnp.full((1, 2 * tq), NEG_INF, F32), unroll=8)

        @pl.when(jnp.logical_and(s >= 1, s <= n_units))
        def _():
            up = s - 1
            tp = lax.shift_right_logical(up, pair_shift)
            ip = up & (N_PAIRS_A - 1)
            ot = acc_scr[...]
            ot = ot[0:AKV_WIDTH] / ot[AKV_WIDTH:AKV_WIDTH + 1]
            oblk = jnp.concatenate([ot[0:HEAD_DIM, 0:tq], ot[HEAD_DIM:2 * HEAD_DIM, tq:2 * tq]], axis=0)
            o_ref[0, ip, pl.ds(pl.multiple_of(tp * tq, tq), tq), :] = oblk.T.astype(BF16)

        return m_cur

    def two_stages(j, m):
        m = stage(2 * j, m, st_a, st_b)
        return stage(2 * j + 1, m, st_b, st_a)

    lax.fori_loop(0, (n_units + 2) // 2, two_stages, jnp.zeros((1, 2 * tq), F32))


def _mix_a(qa, ka, vat):
    B, _, S, _ = qa.shape
    tq = TQ_A
    _, nkc, vrows, kc = vat.shape
    return pl.pallas_call(
        _mixa_body,
        grid=(B,),
        in_specs=[
            pl.BlockSpec((1, N_PAIRS_A, S, LANES), lambda b: (b, 0, 0, 0)),
            pl.BlockSpec((1, S, AKV_WIDTH), lambda b: (b, 0, 0)),
            pl.BlockSpec((1, nkc, vrows, kc), lambda b: (b, 0, 0, 0)),
        ],
        out_specs=pl.BlockSpec((1, N_PAIRS_A, S, LANES), lambda b: (b, 0, 0, 0)),
        out_shape=jax.ShapeDtypeStruct((B, N_PAIRS_A, S, LANES), BF16),
        scratch_shapes=[pltpu.VMEM((S, 2 * tq), F32), pltpu.VMEM((S, 2 * tq), F32), pltpu.VMEM((vrows, 2 * tq), F32),
                        pltpu.VMEM((2 * tq, LANES), BF16)],
        compiler_params=pltpu.CompilerParams(
            dimension_semantics=("arbitrary",), vmem_limit_bytes=VMEM_LIMIT),
        name="mix_a",
    )(qa, ka, vat)


def _mixb_body(q_ref, *rest):
    nbr = len(DILATIONS)
    kv_refs = rest[:2 * nbr]
    bm_ref, o_ref, oacc, macc, dacc = rest[2 * nbr:]
    S = q_ref.shape[1]
    lane = lax.broadcasted_iota(jnp.int32, (1, LANES), 1)
    lo = lane < HEAD_DIM
    ones = jnp.ones((KW, LANES), BF16)
    order = sorted(range(nbr), key=lambda bi: -DILATIONS[bi])
    assert DILATIONS[order[-1]] == 1

    for slot, bi in enumerate(order):
        d = DILATIONS[bi]
        k_ref, v_ref = kv_refs[2 * bi], kv_refs[2 * bi + 1]
        L = S // d
        nc = L // QC
        shift = nc.bit_length() - 1
        last = slot == nbr - 1

        def body(it, carry, bi=bi, d=d, L=L, nc=nc, shift=shift, slot=slot, k_ref=k_ref, v_ref=v_ref, last=last):
            r = lax.shift_right_logical(it, shift)
            c = it & (nc - 1)
            j0 = c * QC
            ws = pl.multiple_of(jnp.clip(j0 - HALF, 0, L - KW), HALF)
            var = jnp.where(c == 0, 0, jnp.where(c == nc - 1, 2, 1))
            trow = pl.ds(pl.multiple_of(j0, QC), QC) if d == 1 else pl.ds(r + d * j0, QC, stride=d)
            qc = q_ref[0, trow, :].astype(BF16)
            kw = k_ref[0, r, pl.ds(ws, KW), :]
            vw = jnp.concatenate([v_ref[0, r, pl.ds(ws, KW), :], ones], axis=1)
            zero = jnp.zeros_like(qc)
            q2 = jnp.concatenate([jnp.where(lo, qc, zero), jnp.where(lo, zero, qc)], axis=0)
            s = lax.dot_general(q2, kw, (((1,), (1,)), ((), ())), preferred_element_type=F32)
            s = s + bm_ref[bi, var].reshape(2 * QC, KW)
            m = jnp.max(s, axis=-1, keepdims=True)
            e = jnp.exp2(s - m).astype(BF16)
            ox = jnp.dot(e, vw, preferred_element_type=F32)
            o = jnp.where(lo, ox[0:QC, 0:LANES], ox[QC:2 * QC, 0:LANES])
            den = jnp.where(lo, ox[0:QC, LANES:2 * LANES], ox[QC:2 * QC, LANES:2 * LANES])
            mb = jnp.where(lo, m[0:QC], m[QC:2 * QC])
            if not last:
                oacc[slot, trow, :] = o
                dacc[slot, trow, :] = den
                macc[slot, trow, :] = mb
            else:
                ms = [macc[sl, trow, :] for sl in range(nbr - 1)] + [mb]
                os_ = [oacc[sl, trow, :] for sl in range(nbr - 1)] + [o]
                ds_ = [dacc[sl, trow, :] for sl in range(nbr - 1)] + [den]
                mm = functools.reduce(jnp.maximum, ms)
                ws_ = [jnp.exp2(mi - mm) for mi in ms]
                num = functools.reduce(lambda a, b: a + b, [w * oi for w, oi in zip(ws_, os_)])
                dsum = functools.reduce(lambda a, b: a + b, [w * di for w, di in zip(ws_, ds_)])
                o_ref[0, trow, :] = (num / dsum).astype(BF16)
            return carry

        lax.fori_loop(0, d * nc, body, 0, unroll=8)


def _mix_b(qb, kvs, bmt):
    B, S, _ = qb.shape
    nbr = len(DILATIONS)
    assert all((S // d) % QC == 0 and S // d >= KW and ((S // d // QC) & (S // d // QC - 1)) == 0 for d in DILATIONS)
    kv_specs = []
    for d in DILATIONS:
        kv_specs.append(pl.BlockSpec((1, d, S // d, LANES), lambda b, p: (b, 0, 0, p)))
        kv_specs.append(pl.BlockSpec((1, d, S // d, LANES), lambda b, p: (b, 0, 0, N_PAIRS_B + p)))
    return pl.pallas_call(
        _mixb_body,
        grid=(B, N_PAIRS_B),
        in_specs=[pl.BlockSpec((1, S, LANES), lambda b, p: (b, 0, p))] + kv_specs + [
            pl.BlockSpec((nbr, N_VARIANTS, 2, QC, KW), lambda b, p: (0, 0, p, 0, 0)),
        ],
        out_specs=pl.BlockSpec((1, S, LANES), lambda b, p: (b, 0, p)),
        out_shape=jax.ShapeDtypeStruct((B, S, B_WIDTH), BF16),
        scratch_shapes=[pltpu.VMEM((nbr - 1, S, LANES), F32)] * 3,
        compiler_params=pltpu.CompilerParams(
            dimension_semantics=("arbitrary", "arbitrary"), vmem_limit_bytes=VMEM_LIMIT),
        name="mix_b",
    )(qb, *[kv for kv in kvs for _ in range(2)], bmt)


def _ffn_body(x_ref, xp_ref, xn_ref, oa_ref, oap_ref, oan_ref, ob_ref, obp_ref, obn_ref,
              wo_ref, g_ref, wup_ref, cw_ref, cb_ref, wd_ref, fn_ref, o_ref, o_scr, *, final):
    t = pl.program_id(1)
    nt = pl.num_programs(1)
    tm = x_ref.shape[1]
    rows = tm + 2 * HALO
    dff = wd_ref.shape[0]

    for i in range(N_PAIRS_A + 1):
        cols = slice(LANES * i, LANES * (i + 1)) if i < N_PAIRS_A else slice(A_WIDTH, A_WIDTH + B_WIDTH)
        o_scr[0:HALO, cols] = oap_ref[0, i] if i < N_PAIRS_A else obp_ref[0]
        o_scr[HALO:HALO + tm, cols] = oa_ref[0, i] if i < N_PAIRS_A else ob_ref[0]
        o_scr[HALO + tm:rows, cols] = oan_ref[0, i] if i < N_PAIRS_A else obn_ref[0]
    x_ext = jnp.concatenate([xp_ref[0], x_ref[0], xn_ref[0]], axis=0)
    x1 = x_ext + jnp.dot(o_scr[...], wo_ref[...], preferred_element_type=F32)
    rid = lax.broadcasted_iota(jnp.int32, (rows, 1), 0)
    inside = jnp.logical_and(jnp.logical_or(t > 0, rid >= HALO), jnp.logical_or(t < nt - 1, rid < HALO + tm))
    h = jnp.where(inside, _rms(x1, g_ref[...]), 0.0).astype(BF16)

    y = x1[HALO:HALO + tm]
    for c0 in range(0, dff, FFN_CHUNK):
        cw = min(FFN_CHUNK, dff - c0)

        def conv(off):
            up = jnp.dot(h, wup_ref[:, off:off + cw], preferred_element_type=F32)
            prev = pltpu.roll(up, 1, 0)[HALO:HALO + tm]
            nxt = pltpu.roll(up, rows - 1, 0)[HALO:HALO + tm]
            mid = up[HALO:HALO + tm]
            return (cb_ref[:, off:off + cw] + prev * cw_ref[0:1, off:off + cw]
                    + mid * cw_ref[1:2, off:off + cw] + nxt * cw_ref[2:3, off:off + cw])

        ug = conv(c0)
        uv = conv(dff + c0)
        act = (ug * (1.0 / (1.0 + jnp.exp(-ug))) * uv).astype(BF16)
        y = y + jnp.dot(act, wd_ref[c0:c0 + cw, :], preferred_element_type=F32)
    o_ref[0] = _rms(y, fn_ref[...]) if final else y


def _ffn(x, oa, ob, w_out, g, w_up, conv_w, conv_b, w_down, fn, final):
    B, S, D = x.shape
    tm = TM_FFN
    assert S % tm == 0 and tm % HALO == 0 and w_down.shape[0] % LANES == 0 and FFN_CHUNK % LANES == 0
    hb = tm // HALO
    nhb = S // HALO

    def prev_blk(t):
        return jnp.maximum(t * hb - 1, 0)

    def next_blk(t):
        return jnp.minimum((t + 1) * hb, nhb - 1)

    def resident(a):
        return pl.BlockSpec(a.shape, lambda b, t: (0,) * a.ndim, pipeline_mode=pl.Buffered(1))

    return pl.pallas_call(
        functools.partial(_ffn_body, final=final),
        grid=(B, S // tm),
        in_specs=[
            pl.BlockSpec((1, tm, D), lambda b, t: (b, t, 0)),
            pl.BlockSpec((1, HALO, D), lambda b, t: (b, prev_blk(t), 0)),
            pl.BlockSpec((1, HALO, D), lambda b, t: (b, next_blk(t), 0)),
            pl.BlockSpec((1, N_PAIRS_A, tm, LANES), lambda b, t: (b, 0, t, 0)),
            pl.BlockSpec((1, N_PAIRS_A, HALO, LANES), lambda b, t: (b, 0, prev_blk(t), 0)),
            pl.BlockSpec((1, N_PAIRS_A, HALO, LANES), lambda b, t: (b, 0, next_blk(t), 0)),
            pl.BlockSpec((1, tm, B_WIDTH), lambda b, t: (b, t, 0)),
            pl.BlockSpec((1, HALO, B_WIDTH), lambda b, t: (b, prev_blk(t), 0)),
            pl.BlockSpec((1, HALO, B_WIDTH), lambda b, t: (b, next_blk(t), 0)),
            resident(w_out), resident(g), resident(w_up), resident(conv_w), resident(conv_b),
            resident(w_down), resident(fn),
        ],
        out_specs=pl.BlockSpec((1, tm, D), lambda b, t: (b, t, 0)),
        out_shape=jax.ShapeDtypeStruct((B, S, D), F32),
        scratch_shapes=[pltpu.VMEM((tm + 2 * HALO, A_WIDTH + B_WIDTH), BF16)],
        compiler_params=pltpu.CompilerParams(
            dimension_semantics=("arbitrary", "arbitrary"), vmem_limit_bytes=VMEM_LIMIT),
        name="ffn",
    )(x, x, x, oa, oa, oa, ob, ob, ob, w_out, g, w_up, conv_w, conv_b, w_down, fn)


def _rope_tables(S):
    rows = S // GRID_W
    row = jnp.repeat(jnp.arange(rows), GRID_W).astype(F32)
    col = jnp.tile(jnp.arange(GRID_W), rows).astype(F32)
    n = HEAD_DIM // 4
    inv = ROPE_THETA ** (-jnp.arange(n, dtype=F32) / n)
    ang = jnp.concatenate([row[:, None] * inv, col[:, None] * inv], axis=-1)
    lane = jnp.arange(LANES)
    pair = (lane % HEAD_DIM) // 2
    sign = jnp.where(lane % 2 == 0, -1.0, 1.0).astype(F32)
    return jnp.cos(ang)[:, pair], jnp.sin(ang)[:, pair] * sign


def _qa_column_order():
    c = jnp.arange(A_WIDTH)
    tile, half, dim = c // LANES, (c % LANES) // HEAD_DIM, c % HEAD_DIM
    return (tile + (A_HEADS // A_KV_HEADS) * half) * HEAD_DIM + dim


def _trunk(x, p):
    depth = p["w_in"].shape[0]
    for l in range(depth):
        qa, ka, vat, qb, *kvs = _in_proj(x, p["attn_norm"][l], p["w_in"][l], p["gq"][l], p["gk"][l], p["cos"], p["sin"])
        oa = _mix_a(qa, ka, vat)
        ob = _mix_b(qb, kvs, p["bias"])
        x = _ffn(x, oa, ob, p["w_out"][l], p["ffn_norm"][l], p["w_up"][l], p["conv_w"][l], p["conv_b"][l],
                 p["w_down"][l], p["final_norm"], final=(l == depth - 1))
    return x


def kernel(x_prompt, x_sample, attn_norm, w_in, q_norm, k_norm, rel_bias, w_out, ffn_norm, w_up, conv_w, conv_b, w_down, final_norm):
    depth, d_model, _ = w_in.shape
    order = _qa_column_order()
    w_in_p = jnp.concatenate([w_in[:, :, order], w_in[:, :, A_WIDTH:]], axis=-1).astype(BF16)
    w_out_p = jnp.concatenate([w_out[:, order, :], w_out[:, A_WIDTH:, :]], axis=1).astype(BF16)
    params = {
        "attn_norm": attn_norm.reshape(depth, 1, d_model),
        "w_in": w_in_p,
        "gq": (jnp.tile(q_norm, (1, 2)) * Q_SCALE).reshape(depth, 1, LANES),
        "gk": jnp.tile(k_norm, (1, 2)).reshape(depth, 1, LANES),
        "w_out": w_out_p,
        "ffn_norm": ffn_norm.reshape(depth, 1, d_model),
        "w_up": w_up.astype(BF16),
        "conv_w": conv_w,
        "conv_b": conv_b.reshape(depth, 1, -1),
        "w_down": w_down.astype(BF16),
        "final_norm": final_norm.reshape(1, d_model),
        "bias": _bias_tiles(rel_bias),
    }
    outs = []
    for x in (x_prompt, x_sample):
        cos2, sin2 = _rope_tables(x.shape[1])
        outs.append(_trunk(x, dict(params, cos=cos2, sin=sin2)))
    return tuple(outs)
```

```python
import functools
import math

import jax
import jax.numpy as jnp
from jax import lax
from jax.experimental import pallas as pl
from jax.experimental.pallas import tpu as pltpu

F32 = jnp.float32
BF16 = jnp.bfloat16

HEAD_DIM = 64
A_HEADS = 8
A_KV_HEADS = 2
B_HEADS = 8
GRID_W = 64
ROPE_THETA = 10000.0
DILATED_BRANCHES = ((128, 1), (512, 4), (2048, 16))
N_BUCKETS = 32
REL_MAX_DIST = 1024
CONV_WIDTH = 3
EPS = 1e-6
NEG_INF = -1e30
ATTN_SCALE = HEAD_DIM ** -0.5
LOG2E = math.log2(math.e)
Q_SCALE = ATTN_SCALE * LOG2E

LANES = 128
A_WIDTH = A_HEADS * HEAD_DIM
AKV_WIDTH = A_KV_HEADS * HEAD_DIM
B_WIDTH = B_HEADS * HEAD_DIM
N_PAIRS_A = A_WIDTH // LANES
N_PAIRS_B = B_WIDTH // LANES

DILATIONS = tuple(d for _, d in DILATED_BRANCHES)
HALF = DILATED_BRANCHES[0][0] // (2 * DILATED_BRANCHES[0][1])
assert all(w // (2 * d) == HALF for w, d in DILATED_BRANCHES)
QC = 128
KW = QC + 2 * HALF
N_VARIANTS = 3

TM_PROJ = 512
TQ_A = 256
TM_FFN = 512
FFN_CHUNK = 512
HALO = 16
ONES_ROWS = 16
VMEM_LIMIT = 56 * 1024 * 1024


def _rms(x, g):
    ms = jnp.mean(x * x, axis=-1, keepdims=True)
    return x * lax.rsqrt(ms + EPS) * g


def _bias_body(tab_ref, out_ref):
    br = pl.program_id(0)
    var = pl.program_id(1)
    dil = jnp.where(br == 0, DILATIONS[0], jnp.where(br == 1, DILATIONS[1], DILATIONS[2]))
    i = lax.broadcasted_iota(jnp.int32, (QC, KW), 0)
    jj = lax.broadcasted_iota(jnp.int32, (QC, KW), 1)
    delta = jj - var * HALF - i
    rel = delta * dil
    nb = N_BUCKETS // 2
    max_exact = nb // 2
    n = jnp.abs(rel)
    large = max_exact + (jnp.log(jnp.maximum(n, 1).astype(F32) / max_exact)
                         / math.log(REL_MAX_DIST / max_exact) * (nb - max_exact)).astype(jnp.int32)
    large = jnp.minimum(large, nb - 1)
    bucket = jnp.where(rel > 0, nb, 0) + jnp.where(n < max_exact, n, large)
    band = jnp.abs(delta) <= HALF
    for h in range(B_HEADS):
        acc = jnp.zeros((QC, KW), F32)
        for b in range(N_BUCKETS):
            acc = jnp.where(bucket == b, tab_ref[b, h], acc)
        out_ref[0, 0, h] = jnp.where(band, acc * LOG2E, NEG_INF)


def _bias_tiles(rel_bias):
    nbr = len(DILATIONS)
    return pl.pallas_call(
        _bias_body,
        grid=(nbr, N_VARIANTS),
        in_specs=[pl.BlockSpec(memory_space=pltpu.SMEM)],
        out_specs=pl.BlockSpec((1, 1, B_HEADS, QC, KW), lambda b, v: (b, v, 0, 0, 0)),
        out_shape=jax.ShapeDtypeStruct((nbr, N_VARIANTS, B_HEADS, QC, KW), F32),
        name="bias_tiles",
    )(rel_bias)


def _inproj_body(x_ref, g_ref, w_ref, gq_ref, gk_ref, cos_ref, sin_ref,
                 qa_ref, ka_ref, vat_ref, qb_ref, *rest):
    kv_refs, kv_scr = rest[:-1], rest[-1]
    h = _rms(x_ref[0], g_ref[...]).astype(BF16)
    lane = lax.broadcasted_iota(jnp.int32, (1, LANES), 1)
    lo = lane < HEAD_DIM
    even = (lane & 1) == 0
    cos = cos_ref[...]
    sin = sin_ref[...]

    def norm_rope(blk, gain):
        sq = blk * blk
        s_lo = jnp.sum(jnp.where(lo, sq, 0.0), axis=-1, keepdims=True)
        s_hi = jnp.sum(jnp.where(lo, 0.0, sq), axis=-1, keepdims=True)
        r = jnp.where(lo, lax.rsqrt(s_lo * (1.0 / HEAD_DIM) + EPS), lax.rsqrt(s_hi * (1.0 / HEAD_DIM) + EPS))
        y = blk * r * gain
        partner = jnp.where(even, pltpu.roll(y, LANES - 1, 1), pltpu.roll(y, 1, 1))
        return y * cos + partner * sin

    na = A_WIDTH + 2 * AKV_WIDTH
    pa = jnp.dot(h, w_ref[:, 0:na], preferred_element_type=F32)
    for i in range(N_PAIRS_A):
        qa_ref[0, i] = norm_rope(pa[:, LANES * i:LANES * (i + 1)], gq_ref[...]).astype(BF16)
    ka_ref[0] = norm_rope(pa[:, A_WIDTH:A_WIDTH + AKV_WIDTH], gk_ref[...]).astype(BF16)
    vat_ref[0, 0, 0:AKV_WIDTH] = pa[:, A_WIDTH + AKV_WIDTH:na].T.astype(BF16)
    vat_ref[0, 0, AKV_WIDTH:AKV_WIDTH + ONES_ROWS] = jnp.ones((ONES_ROWS, x_ref.shape[1]), BF16)
    qb_ref[0] = jnp.dot(h, w_ref[:, na:na + B_WIDTH], preferred_element_type=F32) * Q_SCALE
    tm = x_ref.shape[1]
    step = 2 * LANES
    for c0 in range(0, 2 * B_WIDTH, step):
        pkv = jnp.dot(h, w_ref[:, na + B_WIDTH + c0:na + B_WIDTH + c0 + step], preferred_element_type=F32)
        for half in range(step // LANES):
            ct = c0 // LANES + half
            cols = slice(LANES * ct, LANES * (ct + 1))
            tile = pkv[:, LANES * half:LANES * (half + 1)]
            kv_scr[ct] = tile
            for d, kv_ref in zip(DILATIONS, kv_refs):
                if d == 1:
                    kv_ref[0, 0, :, cols] = tile.astype(BF16)
                else:
                    for r in range(d):
                        kv_ref[0, r, :, cols] = kv_scr[ct, pl.ds(r, tm // d, stride=d), :].astype(BF16)


def _in_proj(x, g, w, gq, gk, cos2, sin2):
    B, S, D = x.shape
    tm = TM_PROJ
    nw = w.shape[1]
    return pl.pallas_call(
        _inproj_body,
        grid=(B, S // tm),
        in_specs=[
            pl.BlockSpec((1, tm, D), lambda b, t: (b, t, 0)),
            pl.BlockSpec((1, D), lambda b, t: (0, 0)),
            pl.BlockSpec((D, nw), lambda b, t: (0, 0)),
            pl.BlockSpec((1, LANES), lambda b, t: (0, 0)),
            pl.BlockSpec((1, LANES), lambda b, t: (0, 0)),
            pl.BlockSpec((tm, LANES), lambda b, t: (t, 0)),
            pl.BlockSpec((tm, LANES), lambda b, t: (t, 0)),
        ],
        out_specs=[
            pl.BlockSpec((1, N_PAIRS_A, tm, LANES), lambda b, t: (b, 0, t, 0)),
            pl.BlockSpec((1, tm, AKV_WIDTH), lambda b, t: (b, t, 0)),
            pl.BlockSpec((1, 1, AKV_WIDTH + ONES_ROWS, tm), lambda b, t: (b, t, 0, 0)),
            pl.BlockSpec((1, tm, B_WIDTH), lambda b, t: (b, t, 0)),
        ] + [pl.BlockSpec((1, d, tm // d, 2 * B_WIDTH), lambda b, t: (b, 0, t, 0)) for d in DILATIONS],
        out_shape=[
            jax.ShapeDtypeStruct((B, N_PAIRS_A, S, LANES), BF16),
            jax.ShapeDtypeStruct((B, S, AKV_WIDTH), BF16),
            jax.ShapeDtypeStruct((B, S // tm, AKV_WIDTH + ONES_ROWS, tm), BF16),
            jax.ShapeDtypeStruct((B, S, B_WIDTH), F32),
        ] + [jax.ShapeDtypeStruct((B, d, S // d, 2 * B_WIDTH), BF16) for d in DILATIONS],
        scratch_shapes=[pltpu.VMEM((2 * N_PAIRS_B, tm, LANES), F32)],
        compiler_params=pltpu.CompilerParams(
            dimension_semantics=("arbitrary", "arbitrary"), vmem_limit_bytes=VMEM_LIMIT),
        name="in_proj",
    )(x, g, w, gq, gk, cos2, sin2)


def _mixa_body(q_ref, k_ref, vt_ref, o_ref, st_a, st_b, acc_a, acc_b, q2_scr):
    S = k_ref.shape[1]
    tq = TQ_A
    kc = vt_ref.shape[3]
    n_units = (S // tq) * N_PAIRS_A
    assert n_units % 2 == 0 and n_units >= 4
    pair_shift = N_PAIRS_A.bit_length() - 1
    lane = lax.broadcasted_iota(jnp.int32, (1, LANES), 1)
    lo = lane < HEAD_DIM

    def finalize(u, acc):
        t = lax.shift_right_logical(u, pair_shift)
        i = u & (N_PAIRS_A - 1)
        ot = acc[...]
        ot = ot[0:AKV_WIDTH] / ot[AKV_WIDTH:AKV_WIDTH + 1]
        oblk = jnp.concatenate([ot[0:HEAD_DIM, 0:tq], ot[HEAD_DIM:2 * HEAD_DIM, tq:2 * tq]], axis=0)
        o_ref[0, i, pl.ds(pl.multiple_of(t * tq, tq), tq), :] = oblk.T.astype(BF16)

    def stage(s, m_prev, st_cur, st_prv, acc_w, acc_r, scores=True, softmax=True, fin=True):
        if fin:
            finalize(s - 2, acc_r)
        if scores:
            t = lax.shift_right_logical(s, pair_shift)
            i = s & (N_PAIRS_A - 1)
            qb = q_ref[0, i, pl.ds(pl.multiple_of(t * tq, tq), tq), :]
            zero = jnp.zeros_like(qb)
            q2_scr[...] = jnp.concatenate([jnp.where(lo, qb, zero), jnp.where(lo, zero, qb)], axis=0)
        if softmax:
            acc_w[...] = jnp.zeros(acc_w.shape, F32)

        def chunk(c, mx):
            rows = pl.ds(pl.multiple_of(c * kc, kc), kc)
            if softmax:
                e = jnp.exp2(st_prv[rows, :] - m_prev).astype(BF16)
            if scores:
                sn = lax.dot_general(k_ref[0, rows, :], q2_scr[...], (((1,), (1,)), ((), ())),
                                     preferred_element_type=F32)
                st_cur[rows, :] = sn
                mx = jnp.maximum(mx, jnp.max(sn, axis=0, keepdims=True))
            if softmax:
                acc_w[...] += jnp.dot(vt_ref[0, c], e, preferred_element_type=F32)
            return mx

        return lax.fori_loop(0, S // kc, chunk, jnp.full((1, 2 * tq), NEG_INF, F32), unroll=True)

    m = stage(0, None, st_a, None, None, None, softmax=False, fin=False)
    m = stage(1, m, st_b, st_a, acc_b, None, fin=False)

    def two_stages(j, m):
        m = stage(2 * j, m, st_a, st_b, acc_a, acc_b)
        return stage(2 * j + 1, m, st_b, st_a, acc_b, acc_a)

    m = lax.fori_loop(1, n_units // 2, two_stages, m)
    stage(n_units, m, None, st_b, acc_a, acc_b, scores=False)
    finalize(n_units - 1, acc_a)


def _mix_a(qa, ka, vat):
    B, _, S, _ = qa.shape
    tq = TQ_A
    _, nkc, vrows, kc = vat.shape
    return pl.pallas_call(
        _mixa_body,
        grid=(B,),
        in_specs=[
            pl.BlockSpec((1, N_PAIRS_A, S, LANES), lambda b: (b, 0, 0, 0)),
            pl.BlockSpec((1, S, AKV_WIDTH), lambda b: (b, 0, 0)),
            pl.BlockSpec((1, nkc, vrows, kc), lambda b: (b, 0, 0, 0)),
        ],
        out_specs=pl.BlockSpec((1, N_PAIRS_A, S, LANES), lambda b: (b, 0, 0, 0)),
        out_shape=jax.ShapeDtypeStruct((B, N_PAIRS_A, S, LANES), BF16),
        scratch_shapes=[pltpu.VMEM((S, 2 * tq), F32)] * 2 + [pltpu.VMEM((vrows, 2 * tq), F32)] * 2
        + [pltpu.VMEM((2 * tq, LANES), BF16)],
        compiler_params=pltpu.CompilerParams(
            dimension_semantics=("arbitrary",), vmem_limit_bytes=VMEM_LIMIT),
        name="mix_a",
    )(qa, ka, vat)


def _mixb_body(q_ref, *rest):
    nbr = len(DILATIONS)
    kv_refs = rest[:2 * nbr]
    bm_ref, o_ref, oacc, macc, dacc = rest[2 * nbr:]
    S = q_ref.shape[1]
    lane = lax.broadcasted_iota(jnp.int32, (1, LANES), 1)
    lo = lane < HEAD_DIM
    ones = jnp.ones((KW, LANES), BF16)
    order = sorted(range(nbr), key=lambda bi: -DILATIONS[bi])
    assert DILATIONS[order[-1]] == 1

    for slot, bi in enumerate(order):
        d = DILATIONS[bi]
        k_ref, v_ref = kv_refs[2 * bi], kv_refs[2 * bi + 1]
        L = S // d
        nc = L // QC
        shift = nc.bit_length() - 1
        last = slot == nbr - 1

        def body(it, carry, bi=bi, d=d, L=L, nc=nc, shift=shift, slot=slot, k_ref=k_ref, v_ref=v_ref, last=last):
            r = lax.shift_right_logical(it, shift)
            c = it & (nc - 1)
            j0 = c * QC
            ws = pl.multiple_of(jnp.clip(j0 - HALF, 0, L - KW), HALF)
            var = jnp.where(c == 0, 0, jnp.where(c == nc - 1, 2, 1))
            trow = pl.ds(pl.multiple_of(j0, QC), QC) if d == 1 else pl.ds(r + d * j0, QC, stride=d)
            qc = q_ref[0, trow, :].astype(BF16)
            kw = k_ref[0, r, pl.ds(ws, KW), :]
            vw = jnp.concatenate([v_ref[0, r, pl.ds(ws, KW), :], ones], axis=1)
            zero = jnp.zeros_like(qc)
            q2 = jnp.concatenate([jnp.where(lo, qc, zero), jnp.where(lo, zero, qc)], axis=0)
            s = lax.dot_general(q2, kw, (((1,), (1,)), ((), ())), preferred_element_type=F32)
            s = s + bm_ref[bi, var].reshape(2 * QC, KW)
            m = jnp.max(s, axis=-1, keepdims=True)
            e = jnp.exp2(s - m).astype(BF16)
            ox = jnp.dot(e, vw, preferred_element_type=F32)
            o = jnp.where(lo, ox[0:QC, 0:LANES], ox[QC:2 * QC, 0:LANES])
            den = jnp.where(lo, ox[0:QC, LANES:2 * LANES], ox[QC:2 * QC, LANES:2 * LANES])
            mb = jnp.where(lo, m[0:QC], m[QC:2 * QC])
            if not last:
                oacc[slot, trow, :] = o
                dacc[slot, trow, :] = den
                macc[slot, trow, :] = mb
            else:
                ms = [macc[sl, trow, :] for sl in range(nbr - 1)] + [mb]
                os_ = [oacc[sl, trow, :] for sl in range(nbr - 1)] + [o]
                ds_ = [dacc[sl, trow, :] for sl in range(nbr - 1)] + [den]
                mm = functools.reduce(jnp.maximum, ms)
                ws_ = [jnp.exp2(mi - mm) for mi in ms]
                num = functools.reduce(lambda a, b: a + b, [w * oi for w, oi in zip(ws_, os_)])
                dsum = functools.reduce(lambda a, b: a + b, [w * di for w, di in zip(ws_, ds_)])
                o_ref[0, trow, :] = (num / dsum).astype(BF16)
            return carry

        lax.fori_loop(0, d * nc, body, 0, unroll=8)


def _mix_b(qb, kvs, bmt):
    B, S, _ = qb.shape
    nbr = len(DILATIONS)
    assert all((S // d) % QC == 0 and S // d >= KW and ((S // d // QC) & (S // d // QC - 1)) == 0 for d in DILATIONS)
    kv_specs = []
    for d in DILATIONS:
        kv_specs.append(pl.BlockSpec((1, d, S // d, LANES), lambda b, p: (b, 0, 0, p)))
        kv_specs.append(pl.BlockSpec((1, d, S // d, LANES), lambda b, p: (b, 0, 0, N_PAIRS_B + p)))
    return pl.pallas_call(
        _mixb_body,
        grid=(B, N_PAIRS_B),
        in_specs=[pl.BlockSpec((1, S, LANES), lambda b, p: (b, 0, p))] + kv_specs + [
            pl.BlockSpec((nbr, N_VARIANTS, 2, QC, KW), lambda b, p: (0, 0, p, 0, 0)),
        ],
        out_specs=pl.BlockSpec((1, S, LANES), lambda b, p: (b, 0, p)),
        out_shape=jax.ShapeDtypeStruct((B, S, B_WIDTH), BF16),
        scratch_shapes=[pltpu.VMEM((nbr - 1, S, LANES), F32)] * 3,
        compiler_params=pltpu.CompilerParams(
            dimension_semantics=("arbitrary", "arbitrary"), vmem_limit_bytes=VMEM_LIMIT),
        name="mix_b",
    )(qb, *[kv for kv in kvs for _ in range(2)], bmt)


def _ffn_body(x_ref, xp_ref, xn_ref, oa_ref, oap_ref, oan_ref, ob_ref, obp_ref, obn_ref,
              wo_ref, g_ref, wup_ref, cw_ref, cb_ref, wd_ref, fn_ref, o_ref, o_scr, *, final):
    t = pl.program_id(1)
    nt = pl.num_programs(1)
    tm = x_ref.shape[1]
    rows = tm + 2 * HALO
    dff = wd_ref.shape[0]

    for i in range(N_PAIRS_A + 1):
        cols = slice(LANES * i, LANES * (i + 1)) if i < N_PAIRS_A else slice(A_WIDTH, A_WIDTH + B_WIDTH)
        o_scr[0:HALO, cols] = oap_ref[0, i] if i < N_PAIRS_A else obp_ref[0]
        o_scr[HALO:HALO + tm, cols] = oa_ref[0, i] if i < N_PAIRS_A else ob_ref[0]
        o_scr[HALO + tm:rows, cols] = oan_ref[0, i] if i < N_PAIRS_A else obn_ref[0]
    x_ext = jnp.concatenate([xp_ref[0], x_ref[0], xn_ref[0]], axis=0)
    x1 = x_ext + jnp.dot(o_scr[...], wo_ref[...], preferred_element_type=F32)
    rid = lax.broadcasted_iota(jnp.int32, (rows, 1), 0)
    inside = jnp.logical_and(jnp.logical_or(t > 0, rid >= HALO), jnp.logical_or(t < nt - 1, rid < HALO + tm))
    h = jnp.where(inside, _rms(x1, g_ref[...]), 0.0).astype(BF16)

    y = x1[HALO:HALO + tm]
    for c0 in range(0, dff, FFN_CHUNK):
        cw = min(FFN_CHUNK, dff - c0)

        def conv(off):
            up = jnp.dot(h, wup_ref[:, off:off + cw], preferred_element_type=F32)
            prev = pltpu.roll(up, 1, 0)[HALO:HALO + tm]
            nxt = pltpu.roll(up, rows - 1, 0)[HALO:HALO + tm]
            mid = up[HALO:HALO + tm]
            return (cb_ref[:, off:off + cw] + prev * cw_ref[0:1, off:off + cw]
                    + mid * cw_ref[1:2, off:off + cw] + nxt * cw_ref[2:3, off:off + cw])

        ug = conv(c0)
        uv = conv(dff + c0)
        act = (ug * (1.0 / (1.0 + jnp.exp(-ug))) * uv).astype(BF16)
        y = y + jnp.dot(act, wd_ref[c0:c0 + cw, :], preferred_element_type=F32)
    o_ref[0] = _rms(y, fn_ref[...]) if final else y


def _ffn(x, oa, ob, w_out, g, w_up, conv_w, conv_b, w_down, fn, final):
    B, S, D = x.shape
    tm = TM_FFN
    assert S % tm == 0 and tm % HALO == 0 and w_down.shape[0] % LANES == 0 and FFN_CHUNK % LANES == 0
    hb = tm // HALO
    nhb = S // HALO

    def prev_blk(t):
        return jnp.maximum(t * hb - 1, 0)

    def next_blk(t):
        return jnp.minimum((t + 1) * hb, nhb - 1)

    def resident(a):
        return pl.BlockSpec(a.shape, lambda b, t: (0,) * a.ndim, pipeline_mode=pl.Buffered(1))

    return pl.pallas_call(
        functools.partial(_ffn_body, final=final),
        grid=(B, S // tm),
        in_specs=[
            pl.BlockSpec((1, tm, D), lambda b, t: (b, t, 0)),
            pl.BlockSpec((1, HALO, D), lambda b, t: (b, prev_blk(t), 0)),
            pl.BlockSpec((1, HALO, D), lambda b, t: (b, next_blk(t), 0)),
            pl.BlockSpec((1, N_PAIRS_A, tm, LANES), lambda b, t: (b, 0, t, 0)),
            pl.BlockSpec((1, N_PAIRS_A, HALO, LANES), lambda b, t: (b, 0, prev_blk(t), 0)),
            pl.BlockSpec((1, N_PAIRS_A, HALO, LANES), lambda b, t: (b, 0, next_blk(t), 0)),
            pl.BlockSpec((1, tm, B_WIDTH), lambda b, t: (b, t, 0)),
            pl.BlockSpec((1, HALO, B_WIDTH), lambda b, t: (b, prev_blk(t), 0)),
            pl.BlockSpec((1, HALO, B_WIDTH), lambda b, t: (b, next_blk(t), 0)),
            resident(w_out), resident(g), resident(w_up), resident(conv_w), resident(conv_b),
            resident(w_down), resident(fn),
        ],
        out_specs=pl.BlockSpec((1, tm, D), lambda b, t: (b, t, 0)),
        out_shape=jax.ShapeDtypeStruct((B, S, D), F32),
        scratch_shapes=[pltpu.VMEM((tm + 2 * HALO, A_WIDTH + B_WIDTH), BF16)],
        compiler_params=pltpu.CompilerParams(
            dimension_semantics=("arbitrary", "arbitrary"), vmem_limit_bytes=VMEM_LIMIT),
        name="ffn",
    )(x, x, x, oa, oa, oa, ob, ob, ob, w_out, g, w_up, conv_w, conv_b, w_down, fn)


def _rope_tables(S):
    rows = S // GRID_W
    row = jnp.repeat(jnp.arange(rows), GRID_W).astype(F32)
    col = jnp.tile(jnp.arange(GRID_W), rows).astype(F32)
    n = HEAD_DIM // 4
    inv = ROPE_THETA ** (-jnp.arange(n, dtype=F32) / n)
    ang = jnp.concatenate([row[:, None] * inv, col[:, None] * inv], axis=-1)
    lane = jnp.arange(LANES)
    pair = (lane % HEAD_DIM) // 2
    sign = jnp.where(lane % 2 == 0, -1.0, 1.0).astype(F32)
    return jnp.cos(ang)[:, pair], jnp.sin(ang)[:, pair] * sign


def _qa_column_order():
    c = jnp.arange(A_WIDTH)
    tile, half, dim = c // LANES, (c % LANES) // HEAD_DIM, c % HEAD_DIM
    return (tile + (A_HEADS // A_KV_HEADS) * half) * HEAD_DIM + dim


def _trunk(x, p):
    depth = p["w_in"].shape[0]
    for l in range(depth):
        qa, ka, vat, qb, *kvs = _in_proj(x, p["attn_norm"][l], p["w_in"][l], p["gq"][l], p["gk"][l], p["cos"], p["sin"])
        oa = _mix_a(qa, ka, vat)
        ob = _mix_b(qb, kvs, p["bias"])
        x = _ffn(x, oa, ob, p["w_out"][l], p["ffn_norm"][l], p["w_up"][l], p["conv_w"][l], p["conv_b"][l],
                 p["w_down"][l], p["final_norm"], final=(l == depth - 1))
    return x


def kernel(x_prompt, x_sample, attn_norm, w_in, q_norm, k_norm, rel_bias, w_out, ffn_norm, w_up, conv_w, conv_b, w_down, final_norm):
    depth, d_model, _ = w_in.shape
    order = _qa_column_order()
    w_in_p = jnp.concatenate([w_in[:, :, order], w_in[:, :, A_WIDTH:]], axis=-1).astype(BF16)
    w_out_p = jnp.concatenate([w_out[:, order, :], w_out[:, A_WIDTH:, :]], axis=1).astype(BF16)
    params = {
        "attn_norm": attn_norm.reshape(depth, 1, d_model),
        "w_in": w_in_p,
        "gq": (jnp.tile(q_norm, (1, 2)) * Q_SCALE).reshape(depth, 1, LANES),
        "gk": jnp.tile(k_norm, (1, 2)).reshape(depth, 1, LANES),
        "w_out": w_out_p,
        "ffn_norm": ffn_norm.reshape(depth, 1, d_model),
        "w_up": w_up.astype(BF16),
        "conv_w": conv_w,
        "conv_b": conv_b.reshape(depth, 1, -1),
        "w_down": w_down.astype(BF16),
        "final_norm": final_norm.reshape(1, d_model),
        "bias": _bias_tiles(rel_bias),
    }
    outs = []
    for x in (x_prompt, x_sample):
        cos2, sin2 = _rope_tables(x.shape[1])
        outs.append(_trunk(x, dict(params, cos=cos2, sin=sin2)))
    return tuple(outs)
```

```python
import functools
import math

import jax
import jax.numpy as jnp
from jax import lax
from jax.experimental import pallas as pl
from jax.experimental.pallas import tpu as pltpu

F32 = jnp.float32
BF16 = jnp.bfloat16

HEAD_DIM = 64
A_HEADS = 8
A_KV_HEADS = 2
B_HEADS = 8
GRID_W = 64
ROPE_THETA = 10000.0
DILATED_BRANCHES = ((128, 1), (512, 4), (2048, 16))
N_BUCKETS = 32
REL_MAX_DIST = 1024
CONV_WIDTH = 3
EPS = 1e-6
NEG_INF = -1e30
ATTN_SCALE = HEAD_DIM ** -0.5
LOG2E = math.log2(math.e)
Q_SCALE = ATTN_SCALE * LOG2E

LANES = 128
A_WIDTH = A_HEADS * HEAD_DIM
AKV_WIDTH = A_KV_HEADS * HEAD_DIM
B_WIDTH = B_HEADS * HEAD_DIM
N_PAIRS_A = A_WIDTH // LANES
N_PAIRS_B = B_WIDTH // LANES

DILATIONS = tuple(d for _, d in DILATED_BRANCHES)
HALF = DILATED_BRANCHES[0][0] // (2 * DILATED_BRANCHES[0][1])
assert all(w // (2 * d) == HALF for w, d in DILATED_BRANCHES)
QC = 128
KW = QC + 2 * HALF
N_VARIANTS = 3

TM_PROJ = 512
TQ_A = 256
TM_FFN = 512
FFN_CHUNK = 1024
HALO = 16
ONES_ROWS = 16
VMEM_LIMIT = 56 * 1024 * 1024


def _rms(x, g):
    ms = jnp.mean(x * x, axis=-1, keepdims=True)
    return x * lax.rsqrt(ms + EPS) * g


def _bias_body(tab_ref, out_ref):
    br = pl.program_id(0)
    var = pl.program_id(1)
    dil = jnp.where(br == 0, DILATIONS[0], jnp.where(br == 1, DILATIONS[1], DILATIONS[2]))
    i = lax.broadcasted_iota(jnp.int32, (QC, KW), 0)
    jj = lax.broadcasted_iota(jnp.int32, (QC, KW), 1)
    delta = jj - var * HALF - i
    rel = delta * dil
    nb = N_BUCKETS // 2
    max_exact = nb // 2
    n = jnp.abs(rel)
    large = max_exact + (jnp.log(jnp.maximum(n, 1).astype(F32) / max_exact)
                         / math.log(REL_MAX_DIST / max_exact) * (nb - max_exact)).astype(jnp.int32)
    large = jnp.minimum(large, nb - 1)
    bucket = jnp.where(rel > 0, nb, 0) + jnp.where(n < max_exact, n, large)
    band = jnp.abs(delta) <= HALF
    for h in range(B_HEADS):
        acc = jnp.zeros((QC, KW), F32)
        for b in range(N_BUCKETS):
            acc = jnp.where(bucket == b, tab_ref[b, h], acc)
        out_ref[0, 0, h] = jnp.where(band, acc * LOG2E, NEG_INF)


def _bias_tiles(rel_bias):
    nbr = len(DILATIONS)
    return pl.pallas_call(
        _bias_body,
        grid=(nbr, N_VARIANTS),
        in_specs=[pl.BlockSpec(memory_space=pltpu.SMEM)],
        out_specs=pl.BlockSpec((1, 1, B_HEADS, QC, KW), lambda b, v: (b, v, 0, 0, 0)),
        out_shape=jax.ShapeDtypeStruct((nbr, N_VARIANTS, B_HEADS, QC, KW), F32),
        name="bias_tiles",
    )(rel_bias)


def _inproj_body(x_ref, g_ref, w_ref, gq_ref, gk_ref, cos_ref, sin_ref,
                 qa_ref, ka_ref, vat_ref, qb_ref, *rest):
    kv_refs, kv_scr = rest[:-1], rest[-1]
    h = _rms(x_ref[0], g_ref[...]).astype(BF16)
    lane = lax.broadcasted_iota(jnp.int32, (1, LANES), 1)
    lo = lane < HEAD_DIM
    even = (lane & 1) == 0
    cos = cos_ref[...]
    sin = sin_ref[...]

    def norm_rope(blk, gain):
        sq = blk * blk
        s_lo = jnp.sum(jnp.where(lo, sq, 0.0), axis=-1, keepdims=True)
        s_hi = jnp.sum(jnp.where(lo, 0.0, sq), axis=-1, keepdims=True)
        r = jnp.where(lo, lax.rsqrt(s_lo * (1.0 / HEAD_DIM) + EPS), lax.rsqrt(s_hi * (1.0 / HEAD_DIM) + EPS))
        y = blk * r * gain
        partner = jnp.where(even, pltpu.roll(y, LANES - 1, 1), pltpu.roll(y, 1, 1))
        return y * cos + partner * sin

    na = A_WIDTH + 2 * AKV_WIDTH
    pa = jnp.dot(h, w_ref[:, 0:na], preferred_element_type=F32)
    for i in range(N_PAIRS_A):
        qa_ref[0, i] = norm_rope(pa[:, LANES * i:LANES * (i + 1)], gq_ref[...]).astype(BF16)
    ka_ref[0] = norm_rope(pa[:, A_WIDTH:A_WIDTH + AKV_WIDTH], gk_ref[...]).astype(BF16)
    vat_ref[0, 0, 0:AKV_WIDTH] = pa[:, A_WIDTH + AKV_WIDTH:na].T.astype(BF16)
    vat_ref[0, 0, AKV_WIDTH:AKV_WIDTH + ONES_ROWS] = jnp.ones((ONES_ROWS, x_ref.shape[1]), BF16)
    qb_ref[0] = jnp.dot(h, w_ref[:, na:na + B_WIDTH], preferred_element_type=F32) * Q_SCALE
    tm = x_ref.shape[1]
    step = 2 * LANES
    for c0 in range(0, 2 * B_WIDTH, step):
        pkv = jnp.dot(h, w_ref[:, na + B_WIDTH + c0:na + B_WIDTH + c0 + step], preferred_element_type=F32)
        for half in range(step // LANES):
            ct = c0 // LANES + half
            cols = slice(LANES * ct, LANES * (ct + 1))
            tile = pkv[:, LANES * half:LANES * (half + 1)]
            kv_scr[ct] = tile
            for d, kv_ref in zip(DILATIONS, kv_refs):
                if d == 1:
                    kv_ref[0, 0, :, cols] = tile.astype(BF16)
                else:
                    for r in range(d):
                        kv_ref[0, r, :, cols] = kv_scr[ct, pl.ds(r, tm // d, stride=d), :].astype(BF16)


def _in_proj(x, g, w, gq, gk, cos2, sin2):
    B, S, D = x.shape
    tm = TM_PROJ
    nw = w.shape[1]
    return pl.pallas_call(
        _inproj_body,
        grid=(B, S // tm),
        in_specs=[
            pl.BlockSpec((1, tm, D), lambda b, t: (b, t, 0)),
            pl.BlockSpec((1, D), lambda b, t: (0, 0)),
            pl.BlockSpec((D, nw), lambda b, t: (0, 0)),
            pl.BlockSpec((1, LANES), lambda b, t: (0, 0)),
            pl.BlockSpec((1, LANES), lambda b, t: (0, 0)),
            pl.BlockSpec((tm, LANES), lambda b, t: (t, 0)),
            pl.BlockSpec((tm, LANES), lambda b, t: (t, 0)),
        ],
        out_specs=[
            pl.BlockSpec((1, N_PAIRS_A, tm, LANES), lambda b, t: (b, 0, t, 0)),
            pl.BlockSpec((1, tm, AKV_WIDTH), lambda b, t: (b, t, 0)),
            pl.BlockSpec((1, 1, AKV_WIDTH + ONES_ROWS, tm), lambda b, t: (b, t, 0, 0)),
            pl.BlockSpec((1, tm, B_WIDTH), lambda b, t: (b, t, 0)),
        ] + [pl.BlockSpec((1, d, tm // d, 2 * B_WIDTH), lambda b, t: (b, 0, t, 0)) for d in DILATIONS],
        out_shape=[
            jax.ShapeDtypeStruct((B, N_PAIRS_A, S, LANES), BF16),
            jax.ShapeDtypeStruct((B, S, AKV_WIDTH), BF16),
            jax.ShapeDtypeStruct((B, S // tm, AKV_WIDTH + ONES_ROWS, tm), BF16),
            jax.ShapeDtypeStruct((B, S, B_WIDTH), F32),
        ] + [jax.ShapeDtypeStruct((B, d, S // d, 2 * B_WIDTH), BF16) for d in DILATIONS],
        scratch_shapes=[pltpu.VMEM((2 * N_PAIRS_B, tm, LANES), F32)],
        compiler_params=pltpu.CompilerParams(
            dimension_semantics=("arbitrary", "arbitrary"), vmem_limit_bytes=VMEM_LIMIT),
        name="in_proj",
    )(x, g, w, gq, gk, cos2, sin2)


def _mixa_body(q_ref, k_ref, vt_ref, o_ref, st_a, st_b, acc_a, acc_b, q2_scr):
    S = k_ref.shape[1]
    tq = TQ_A
    kc = vt_ref.shape[3]
    n_units = (S // tq) * N_PAIRS_A
    assert n_units % 2 == 0 and n_units >= 4
    pair_shift = N_PAIRS_A.bit_length() - 1
    lane = lax.broadcasted_iota(jnp.int32, (1, LANES), 1)
    lo = lane < HEAD_DIM

    def finalize(u, acc):
        t = lax.shift_right_logical(u, pair_shift)
        i = u & (N_PAIRS_A - 1)
        ot = acc[...]
        ot = ot[0:AKV_WIDTH] / ot[AKV_WIDTH:AKV_WIDTH + 1]
        oblk = jnp.concatenate([ot[0:HEAD_DIM, 0:tq], ot[HEAD_DIM:2 * HEAD_DIM, tq:2 * tq]], axis=0)
        o_ref[0, i, pl.ds(pl.multiple_of(t * tq, tq), tq), :] = oblk.T.astype(BF16)

    def stage(s, m_prev, st_cur, st_prv, acc_w, acc_r, scores=True, softmax=True, fin=True):
        if fin:
            finalize(s - 2, acc_r)
        if scores:
            t = lax.shift_right_logical(s, pair_shift)
            i = s & (N_PAIRS_A - 1)
            qb = q_ref[0, i, pl.ds(pl.multiple_of(t * tq, tq), tq), :]
            zero = jnp.zeros_like(qb)
            q2_scr[...] = jnp.concatenate([jnp.where(lo, qb, zero), jnp.where(lo, zero, qb)], axis=0)
        if softmax:
            acc_w[...] = jnp.zeros(acc_w.shape, F32)

        def chunk(c, mx):
            rows = pl.ds(pl.multiple_of(c * kc, kc), kc)
            if softmax:
                e = jnp.exp2(st_prv[rows, :] - m_prev).astype(BF16)
            if scores:
                sn = lax.dot_general(k_ref[0, rows, :], q2_scr[...], (((1,), (1,)), ((), ())),
                                     preferred_element_type=F32)
                st_cur[rows, :] = sn
                mx = jnp.maximum(mx, jnp.max(sn, axis=0, keepdims=True))
            if softmax:
                acc_w[...] += jnp.dot(vt_ref[0, c], e, preferred_element_type=F32)
            return mx

        return lax.fori_loop(0, S // kc, chunk, jnp.full((1, 2 * tq), NEG_INF, F32), unroll=True)

    m = stage(0, None, st_a, None, None, None, softmax=False, fin=False)
    m = stage(1, m, st_b, st_a, acc_b, None, fin=False)

    def two_stages(j, m):
        m = stage(2 * j, m, st_a, st_b, acc_a, acc_b)
        return stage(2 * j + 1, m, st_b, st_a, acc_b, acc_a)

    m = lax.fori_loop(1, n_units // 2, two_stages, m)
    stage(n_units, m, None, st_b, acc_a, acc_b, scores=False)
    finalize(n_units - 1, acc_a)


def _mix_a(qa, ka, vat):
    B, _, S, _ = qa.shape
    tq = TQ_A
    _, nkc, vrows, kc = vat.shape
    return pl.pallas_call(
        _mixa_body,
        grid=(B,),
        in_specs=[
            pl.BlockSpec((1, N_PAIRS_A, S, LANES), lambda b: (b, 0, 0, 0)),
            pl.BlockSpec((1, S, AKV_WIDTH), lambda b: (b, 0, 0)),
            pl.BlockSpec((1, nkc, vrows, kc), lambda b: (b, 0, 0, 0)),
        ],
        out_specs=pl.BlockSpec((1, N_PAIRS_A, S, LANES), lambda b: (b, 0, 0, 0)),
        out_shape=jax.ShapeDtypeStruct((B, N_PAIRS_A, S, LANES), BF16),
        scratch_shapes=[pltpu.VMEM((S, 2 * tq), F32)] * 2 + [pltpu.VMEM((vrows, 2 * tq), F32)] * 2
        + [pltpu.VMEM((2 * tq, LANES), BF16)],
        compiler_params=pltpu.CompilerParams(
            dimension_semantics=("arbitrary",), vmem_limit_bytes=VMEM_LIMIT),
        name="mix_a",
    )(qa, ka, vat)


def _mixb_body(q_ref, *rest):
    nbr = len(DILATIONS)
    kv_refs = rest[:2 * nbr]
    bm_ref, o_ref, oacc, macc, dacc = rest[2 * nbr:]
    S = q_ref.shape[1]
    lane = lax.broadcasted_iota(jnp.int32, (1, LANES), 1)
    lo = lane < HEAD_DIM
    ones = jnp.ones((KW, LANES), BF16)
    order = sorted(range(nbr), key=lambda bi: -DILATIONS[bi])
    assert DILATIONS[order[-1]] == 1

    for slot, bi in enumerate(order):
        d = DILATIONS[bi]
        k_ref, v_ref = kv_refs[2 * bi], kv_refs[2 * bi + 1]
        L = S // d
        nc = L // QC
        shift = nc.bit_length() - 1
        last = slot == nbr - 1

        def body(it, carry, bi=bi, d=d, L=L, nc=nc, shift=shift, slot=slot, k_ref=k_ref, v_ref=v_ref, last=last):
            r = lax.shift_right_logical(it, shift)
            c = it & (nc - 1)
            j0 = c * QC
            ws = pl.multiple_of(jnp.clip(j0 - HALF, 0, L - KW), HALF)
            var = jnp.where(c == 0, 0, jnp.where(c == nc - 1, 2, 1))
            trow = pl.ds(pl.multiple_of(j0, QC), QC) if d == 1 else pl.ds(r + d * j0, QC, stride=d)
            qc = q_ref[0, trow, :].astype(BF16)
            kw = k_ref[0, r, pl.ds(ws, KW), :]
            vw = jnp.concatenate([v_ref[0, r, pl.ds(ws, KW), :], ones], axis=1)
            zero = jnp.zeros_like(qc)
            q2 = jnp.concatenate([jnp.where(lo, qc, zero), jnp.where(lo, zero, qc)], axis=0)
            s = lax.dot_general(q2, kw, (((1,), (1,)), ((), ())), preferred_element_type=F32)
            s = s + bm_ref[bi, var].reshape(2 * QC, KW)
            m = jnp.max(s, axis=-1, keepdims=True)
            e = jnp.exp2(s - m).astype(BF16)
            ox = jnp.dot(e, vw, preferred_element_type=F32)
            o = jnp.where(lo, ox[0:QC, 0:LANES], ox[QC:2 * QC, 0:LANES])
            den = jnp.where(lo, ox[0:QC, LANES:2 * LANES], ox[QC:2 * QC, LANES:2 * LANES])
            mb = jnp.where(lo, m[0:QC], m[QC:2 * QC])
            if not last:
                oacc[slot, trow, :] = o
                dacc[slot, trow, :] = den
                macc[slot, trow, :] = mb
            else:
                ms = [macc[sl, trow, :] for sl in range(nbr - 1)] + [mb]
                os_ = [oacc[sl, trow, :] for sl in range(nbr - 1)] + [o]
                ds_ = [dacc[sl, trow, :] for sl in range(nbr - 1)] + [den]
                mm = functools.reduce(jnp.maximum, ms)
                ws_ = [jnp.exp2(mi - mm) for mi in ms]
                num = functools.reduce(lambda a, b: a + b, [w * oi for w, oi in zip(ws_, os_)])
                dsum = functools.reduce(lambda a, b: a + b, [w * di for w, di in zip(ws_, ds_)])
                o_ref[0, trow, :] = (num / dsum).astype(BF16)
            return carry

        lax.fori_loop(0, d * nc, body, 0, unroll=8)


def _mix_b(qb, kvs, bmt):
    B, S, _ = qb.shape
    nbr = len(DILATIONS)
    assert all((S // d) % QC == 0 and S // d >= KW and ((S // d // QC) & (S // d // QC - 1)) == 0 for d in DILATIONS)
    kv_specs = []
    for d in DILATIONS:
        kv_specs.append(pl.BlockSpec((1, d, S // d, LANES), lambda b, p: (b, 0, 0, p)))
        kv_specs.append(pl.BlockSpec((1, d, S // d, LANES), lambda b, p: (b, 0, 0, N_PAIRS_B + p)))
    return pl.pallas_call(
        _mixb_body,
        grid=(B, N_PAIRS_B),
        in_specs=[pl.BlockSpec((1, S, LANES), lambda b, p: (b, 0, p))] + kv_specs + [
            pl.BlockSpec((nbr, N_VARIANTS, 2, QC, KW), lambda b, p: (0, 0, p, 0, 0)),
        ],
        out_specs=pl.BlockSpec((1, S, LANES), lambda b, p: (b, 0, p)),
        out_shape=jax.ShapeDtypeStruct((B, S, B_WIDTH), BF16),
        scratch_shapes=[pltpu.VMEM((nbr - 1, S, LANES), F32)] * 3,
        compiler_params=pltpu.CompilerParams(
            dimension_semantics=("arbitrary", "arbitrary"), vmem_limit_bytes=VMEM_LIMIT),
        name="mix_b",
    )(qb, *[kv for kv in kvs for _ in range(2)], bmt)


def _ffn_body(x_ref, xp_ref, xn_ref, oa_ref, oap_ref, oan_ref, ob_ref, obp_ref, obn_ref,
              wo_ref, g_ref, wup_ref, cw_ref, cb_ref, wd_ref, fn_ref, o_ref, o_scr, up_scr, y_scr, *, final):
    t = pl.program_id(1)
    nt = pl.num_programs(1)
    tm = x_ref.shape[1]
    rows = tm + 2 * HALO
    dff = wd_ref.shape[0]

    for i in range(N_PAIRS_A + 1):
        cols = slice(LANES * i, LANES * (i + 1)) if i < N_PAIRS_A else slice(A_WIDTH, A_WIDTH + B_WIDTH)
        o_scr[0:HALO, cols] = oap_ref[0, i] if i < N_PAIRS_A else obp_ref[0]
        o_scr[HALO:HALO + tm, cols] = oa_ref[0, i] if i < N_PAIRS_A else ob_ref[0]
        o_scr[HALO + tm:rows, cols] = oan_ref[0, i] if i < N_PAIRS_A else obn_ref[0]
    x_ext = jnp.concatenate([xp_ref[0], x_ref[0], xn_ref[0]], axis=0)
    x1 = x_ext + jnp.dot(o_scr[...], wo_ref[...], preferred_element_type=F32)
    rid = lax.broadcasted_iota(jnp.int32, (rows, 1), 0)
    inside = jnp.logical_and(jnp.logical_or(t > 0, rid >= HALO), jnp.logical_or(t < nt - 1, rid < HALO + tm))
    h = jnp.where(inside, _rms(x1, g_ref[...]), 0.0).astype(BF16)

    half = tm // 2
    nlt = FFN_CHUNK // LANES
    y = None
    for ci, c0 in enumerate(range(0, dff, FFN_CHUNK)):
        cw = min(FFN_CHUNK, dff - c0)

        def conv(off, gv):
            up = jnp.dot(h, wup_ref[:, off:off + cw], preferred_element_type=F32)
            tiles = []
            for lt in range(cw // LANES):
                buf = up_scr.at[(ci % 2) * 2 * nlt + gv * nlt + lt]
                buf[...] = up[:, LANES * lt:LANES * (lt + 1)]
                win = [buf[pl.ds(HALO - 1 + k, half, stride=2), :] for k in range(4)]
                prev = jnp.concatenate([win[0], win[1]], axis=0)
                mid = jnp.concatenate([win[1], win[2]], axis=0)
                nxt = jnp.concatenate([win[2], win[3]], axis=0)
                cols = slice(off + LANES * lt, off + LANES * (lt + 1))
                tiles.append(cb_ref[:, cols] + prev * cw_ref[0:1, cols] + mid * cw_ref[1:2, cols]
                             + nxt * cw_ref[2:3, cols])
            return jnp.concatenate(tiles, axis=1)

        ug = conv(c0, 0)
        uv = conv(dff + c0, 1)
        act = (ug * (1.0 / (1.0 + jnp.exp(-ug))) * uv).astype(BF16)
        part = jnp.dot(act, wd_ref[c0:c0 + cw, :], preferred_element_type=F32)
        y = part if y is None else y + part
    outs = []
    for lt in range(y.shape[1] // LANES):
        cols = slice(LANES * lt, LANES * (lt + 1))
        y_scr[lt, pl.ds(0, half, stride=2), :] = y[0:half, cols]
        y_scr[lt, pl.ds(1, half, stride=2), :] = y[half:tm, cols]
        outs.append(y_scr[lt])
    y = x1[HALO:HALO + tm] + jnp.concatenate(outs, axis=1)
    o_ref[0] = _rms(y, fn_ref[...]) if final else y


def _ffn(x, oa, ob, w_out, g, w_up, conv_w, conv_b, w_down, fn, final):
    B, S, D = x.shape
    tm = TM_FFN
    assert S % tm == 0 and tm % HALO == 0 and w_down.shape[0] % LANES == 0 and FFN_CHUNK % LANES == 0
    hb = tm // HALO
    nhb = S // HALO

    def prev_blk(t):
        return jnp.maximum(t * hb - 1, 0)

    def next_blk(t):
        return jnp.minimum((t + 1) * hb, nhb - 1)

    def resident(a):
        return pl.BlockSpec(a.shape, lambda b, t: (0,) * a.ndim, pipeline_mode=pl.Buffered(1))

    return pl.pallas_call(
        functools.partial(_ffn_body, final=final),
        grid=(B, S // tm),
        in_specs=[
            pl.BlockSpec((1, tm, D), lambda b, t: (b, t, 0)),
            pl.BlockSpec((1, HALO, D), lambda b, t: (b, prev_blk(t), 0)),
            pl.BlockSpec((1, HALO, D), lambda b, t: (b, next_blk(t), 0)),
            pl.BlockSpec((1, N_PAIRS_A, tm, LANES), lambda b, t: (b, 0, t, 0)),
            pl.BlockSpec((1, N_PAIRS_A, HALO, LANES), lambda b, t: (b, 0, prev_blk(t), 0)),
            pl.BlockSpec((1, N_PAIRS_A, HALO, LANES), lambda b, t: (b, 0, next_blk(t), 0)),
            pl.BlockSpec((1, tm, B_WIDTH), lambda b, t: (b, t, 0)),
            pl.BlockSpec((1, HALO, B_WIDTH), lambda b, t: (b, prev_blk(t), 0)),
            pl.BlockSpec((1, HALO, B_WIDTH), lambda b, t: (b, next_blk(t), 0)),
            resident(w_out), resident(g), resident(w_up), resident(conv_w), resident(conv_b),
            resident(w_down), resident(fn),
        ],
        out_specs=pl.BlockSpec((1, tm, D), lambda b, t: (b, t, 0)),
        out_shape=jax.ShapeDtypeStruct((B, S, D), F32),
        scratch_shapes=[pltpu.VMEM((tm + 2 * HALO, A_WIDTH + B_WIDTH), BF16),
                        pltpu.VMEM((4 * (FFN_CHUNK // LANES), tm + 2 * HALO, LANES), F32),
                        pltpu.VMEM((D // LANES, tm, LANES), F32)],
        compiler_params=pltpu.CompilerParams(
            dimension_semantics=("arbitrary", "arbitrary"), vmem_limit_bytes=VMEM_LIMIT),
        name="ffn",
    )(x, x, x, oa, oa, oa, ob, ob, ob, w_out, g, w_up, conv_w, conv_b, w_down, fn)


def _rope_tables(S):
    rows = S // GRID_W
    row = jnp.repeat(jnp.arange(rows), GRID_W).astype(F32)
    col = jnp.tile(jnp.arange(GRID_W), rows).astype(F32)
    n = HEAD_DIM // 4
    inv = ROPE_THETA ** (-jnp.arange(n, dtype=F32) / n)
    ang = jnp.concatenate([row[:, None] * inv, col[:, None] * inv], axis=-1)
    lane = jnp.arange(LANES)
    pair = (lane % HEAD_DIM) // 2
    sign = jnp.where(lane % 2 == 0, -1.0, 1.0).astype(F32)
    return jnp.cos(ang)[:, pair], jnp.sin(ang)[:, pair] * sign


def _qa_column_order():
    c = jnp.arange(A_WIDTH)
    tile, half, dim = c // LANES, (c % LANES) // HEAD_DIM, c % HEAD_DIM
    return (tile + (A_HEADS // A_KV_HEADS) * half) * HEAD_DIM + dim


def _trunk(x, p):
    depth = p["w_in"].shape[0]
    for l in range(depth):
        qa, ka, vat, qb, *kvs = _in_proj(x, p["attn_norm"][l], p["w_in"][l], p["gq"][l], p["gk"][l], p["cos"], p["sin"])
        oa = _mix_a(qa, ka, vat)
        ob = _mix_b(qb, kvs, p["bias"])
        x = _ffn(x, oa, ob, p["w_out"][l], p["ffn_norm"][l], p["w_up"][l], p["conv_w"][l], p["conv_b"][l],
                 p["w_down"][l], p["final_norm"], final=(l == depth - 1))
    return x


def kernel(x_prompt, x_sample, attn_norm, w_in, q_norm, k_norm, rel_bias, w_out, ffn_norm, w_up, conv_w, conv_b, w_down, final_norm):
    depth, d_model, _ = w_in.shape
    order = _qa_column_order()
    w_in_p = jnp.concatenate([w_in[:, :, order], w_in[:, :, A_WIDTH:]], axis=-1).astype(BF16)
    w_out_p = jnp.concatenate([w_out[:, order, :], w_out[:, A_WIDTH:, :]], axis=1).astype(BF16)
    params = {
        "attn_norm": attn_norm.reshape(depth, 1, d_model),
        "w_in": w_in_p,
        "gq": (jnp.tile(q_norm, (1, 2)) * Q_SCALE).reshape(depth, 1, LANES),
        "gk": jnp.tile(k_norm, (1, 2)).reshape(depth, 1, LANES),
        "w_out": w_out_p,
        "ffn_norm": ffn_norm.reshape(depth, 1, d_model),
        "w_up": w_up.astype(BF16),
        "conv_w": conv_w,
        "conv_b": conv_b.reshape(depth, 1, -1),
        "w_down": w_down.astype(BF16),
        "final_norm": final_norm.reshape(1, d_model),
        "bias": _bias_tiles(rel_bias),
    }
    outs = []
    for x in (x_prompt, x_sample):
        cos2, sin2 = _rope_tables(x.shape[1])
        outs.append(_trunk(x, dict(params, cos=cos2, sin=sin2)))
    return tuple(outs)
```

```python
import functools
import math

import jax
import jax.numpy as jnp
from jax import lax
from jax.experimental import pallas as pl
from jax.experimental.pallas import tpu as pltpu

F32 = jnp.float32
BF16 = jnp.bfloat16

HEAD_DIM = 64
A_HEADS = 8
A_KV_HEADS = 2
B_HEADS = 8
GRID_W = 64
ROPE_THETA = 10000.0
DILATED_BRANCHES = ((128, 1), (512, 4), (2048, 16))
N_BUCKETS = 32
REL_MAX_DIST = 1024
CONV_WIDTH = 3
EPS = 1e-6
NEG_INF = -1e30
ATTN_SCALE = HEAD_DIM ** -0.5
LOG2E = math.log2(math.e)
Q_SCALE = ATTN_SCALE * LOG2E

LANES = 128
A_WIDTH = A_HEADS * HEAD_DIM
AKV_WIDTH = A_KV_HEADS * HEAD_DIM
B_WIDTH = B_HEADS * HEAD_DIM
N_PAIRS_A = A_WIDTH // LANES
N_PAIRS_B = B_WIDTH // LANES

DILATIONS = tuple(d for _, d in DILATED_BRANCHES)
HALF = DILATED_BRANCHES[0][0] // (2 * DILATED_BRANCHES[0][1])
assert all(w // (2 * d) == HALF for w, d in DILATED_BRANCHES)
QC = 128
KW = QC + 2 * HALF
N_VARIANTS = 3

TM_PROJ = 512
TQ_A = 256
TM_FFN = 512
FFN_CHUNK = 1024
HALO = 16
ONES_ROWS = 16
VMEM_LIMIT = 56 * 1024 * 1024


def _rms(x, g):
    ms = jnp.mean(x * x, axis=-1, keepdims=True)
    return x * lax.rsqrt(ms + EPS) * g


def _bias_body(tab_ref, out_ref):
    br = pl.program_id(0)
    var = pl.program_id(1)
    dil = jnp.where(br == 0, DILATIONS[0], jnp.where(br == 1, DILATIONS[1], DILATIONS[2]))
    i = lax.broadcasted_iota(jnp.int32, (QC, KW), 0)
    jj = lax.broadcasted_iota(jnp.int32, (QC, KW), 1)
    delta = jj - var * HALF - i
    rel = delta * dil
    nb = N_BUCKETS // 2
    max_exact = nb // 2
    n = jnp.abs(rel)
    large = max_exact + (jnp.log(jnp.maximum(n, 1).astype(F32) / max_exact)
                         / math.log(REL_MAX_DIST / max_exact) * (nb - max_exact)).astype(jnp.int32)
    large = jnp.minimum(large, nb - 1)
    bucket = jnp.where(rel > 0, nb, 0) + jnp.where(n < max_exact, n, large)
    band = jnp.abs(delta) <= HALF
    for h in range(B_HEADS):
        acc = jnp.zeros((QC, KW), F32)
        for b in range(N_BUCKETS):
            acc = jnp.where(bucket == b, tab_ref[b, h], acc)
        out_ref[0, 0, h] = jnp.where(band, acc * LOG2E, NEG_INF)


def _bias_tiles(rel_bias):
    nbr = len(DILATIONS)
    return pl.pallas_call(
        _bias_body,
        grid=(nbr, N_VARIANTS),
        in_specs=[pl.BlockSpec(memory_space=pltpu.SMEM)],
        out_specs=pl.BlockSpec((1, 1, B_HEADS, QC, KW), lambda b, v: (b, v, 0, 0, 0)),
        out_shape=jax.ShapeDtypeStruct((nbr, N_VARIANTS, B_HEADS, QC, KW), F32),
        name="bias_tiles",
    )(rel_bias)


def _inproj_body(x_ref, g_ref, w_ref, gq_ref, gk_ref, cos_ref, sin_ref,
                 qa_ref, ka_ref, vat_ref, qb_ref, *rest):
    kv_refs, kv_scr = rest[:-1], rest[-1]
    h = _rms(x_ref[0], g_ref[...]).astype(BF16)
    lane = lax.broadcasted_iota(jnp.int32, (1, LANES), 1)
    lo = lane < HEAD_DIM
    even = (lane & 1) == 0
    cos = cos_ref[...]
    sin = sin_ref[...]

    def norm_rope(blk, gain):
        sq = blk * blk
        s_lo = jnp.sum(jnp.where(lo, sq, 0.0), axis=-1, keepdims=True)
        s_hi = jnp.sum(jnp.where(lo, 0.0, sq), axis=-1, keepdims=True)
        r = jnp.where(lo, lax.rsqrt(s_lo * (1.0 / HEAD_DIM) + EPS), lax.rsqrt(s_hi * (1.0 / HEAD_DIM) + EPS))
        y = blk * r * gain
        partner = jnp.where(even, pltpu.roll(y, LANES - 1, 1), pltpu.roll(y, 1, 1))
        return y * cos + partner * sin

    na = A_WIDTH + 2 * AKV_WIDTH
    pa = jnp.dot(h, w_ref[:, 0:na], preferred_element_type=F32)
    for i in range(N_PAIRS_A):
        qa_ref[0, i] = norm_rope(pa[:, LANES * i:LANES * (i + 1)], gq_ref[...]).astype(BF16)
    ka_ref[0] = norm_rope(pa[:, A_WIDTH:A_WIDTH + AKV_WIDTH], gk_ref[...]).astype(BF16)
    vat_ref[0, 0, 0:AKV_WIDTH] = pa[:, A_WIDTH + AKV_WIDTH:na].T.astype(BF16)
    vat_ref[0, 0, AKV_WIDTH:AKV_WIDTH + ONES_ROWS] = jnp.ones((ONES_ROWS, x_ref.shape[1]), BF16)
    qb_ref[0] = jnp.dot(h, w_ref[:, na:na + B_WIDTH], preferred_element_type=F32) * Q_SCALE
    tm = x_ref.shape[1]
    step = 2 * LANES
    for c0 in range(0, 2 * B_WIDTH, step):
        pkv = jnp.dot(h, w_ref[:, na + B_WIDTH + c0:na + B_WIDTH + c0 + step], preferred_element_type=F32)
        for half in range(step // LANES):
            ct = c0 // LANES + half
            cols = slice(LANES * ct, LANES * (ct + 1))
            tile = pkv[:, LANES * half:LANES * (half + 1)]
            kv_scr[ct] = tile
            for d, kv_ref in zip(DILATIONS, kv_refs):
                if d == 1:
                    kv_ref[0, 0, :, cols] = tile.astype(BF16)
                else:
                    for r in range(d):
                        kv_ref[0, r, :, cols] = kv_scr[ct, pl.ds(r, tm // d, stride=d), :].astype(BF16)


def _in_proj(x, g, w, gq, gk, cos2, sin2):
    B, S, D = x.shape
    tm = TM_PROJ
    nw = w.shape[1]
    return pl.pallas_call(
        _inproj_body,
        grid=(B, S // tm),
        in_specs=[
            pl.BlockSpec((1, tm, D), lambda b, t: (b, t, 0)),
            pl.BlockSpec((1, D), lambda b, t: (0, 0)),
            pl.BlockSpec((D, nw), lambda b, t: (0, 0)),
            pl.BlockSpec((1, LANES), lambda b, t: (0, 0)),
            pl.BlockSpec((1, LANES), lambda b, t: (0, 0)),
            pl.BlockSpec((tm, LANES), lambda b, t: (t, 0)),
            pl.BlockSpec((tm, LANES), lambda b, t: (t, 0)),
        ],
        out_specs=[
            pl.BlockSpec((1, N_PAIRS_A, tm, LANES), lambda b, t: (b, 0, t, 0)),
            pl.BlockSpec((1, tm, AKV_WIDTH), lambda b, t: (b, t, 0)),
            pl.BlockSpec((1, 1, AKV_WIDTH + ONES_ROWS, tm), lambda b, t: (b, t, 0, 0)),
            pl.BlockSpec((1, tm, B_WIDTH), lambda b, t: (b, t, 0)),
        ] + [pl.BlockSpec((1, d, tm // d, 2 * B_WIDTH), lambda b, t: (b, 0, t, 0)) for d in DILATIONS],
        out_shape=[
            jax.ShapeDtypeStruct((B, N_PAIRS_A, S, LANES), BF16),
            jax.ShapeDtypeStruct((B, S, AKV_WIDTH), BF16),
            jax.ShapeDtypeStruct((B, S // tm, AKV_WIDTH + ONES_ROWS, tm), BF16),
            jax.ShapeDtypeStruct((B, S, B_WIDTH), F32),
        ] + [jax.ShapeDtypeStruct((B, d, S // d, 2 * B_WIDTH), BF16) for d in DILATIONS],
        scratch_shapes=[pltpu.VMEM((2 * N_PAIRS_B, tm, LANES), F32)],
        compiler_params=pltpu.CompilerParams(
            dimension_semantics=("arbitrary", "arbitrary"), vmem_limit_bytes=VMEM_LIMIT),
        name="in_proj",
    )(x, g, w, gq, gk, cos2, sin2)


def _mixa_body(q_ref, k_ref, vt_ref, o_ref, st_a, st_b, acc_a, acc_b, q2_scr):
    S = k_ref.shape[1]
    tq = TQ_A
    kc = vt_ref.shape[3]
    n_units = (S // tq) * N_PAIRS_A
    assert n_units % 2 == 0 and n_units >= 4
    pair_shift = N_PAIRS_A.bit_length() - 1
    lane = lax.broadcasted_iota(jnp.int32, (1, LANES), 1)
    lo = lane < HEAD_DIM

    def finalize(u, acc):
        t = lax.shift_right_logical(u, pair_shift)
        i = u & (N_PAIRS_A - 1)
        ot = acc[...]
        ot = ot[0:AKV_WIDTH] / ot[AKV_WIDTH:AKV_WIDTH + 1]
        oblk = jnp.concatenate([ot[0:HEAD_DIM, 0:tq], ot[HEAD_DIM:2 * HEAD_DIM, tq:2 * tq]], axis=0)
        o_ref[0, i, pl.ds(pl.multiple_of(t * tq, tq), tq), :] = oblk.T.astype(BF16)

    def stage(s, m_prev, st_cur, st_prv, acc_w, acc_r, scores=True, softmax=True, fin=True):
        if fin:
            finalize(s - 2, acc_r)
        if scores:
            t = lax.shift_right_logical(s, pair_shift)
            i = s & (N_PAIRS_A - 1)
            qb = q_ref[0, i, pl.ds(pl.multiple_of(t * tq, tq), tq), :]
            zero = jnp.zeros_like(qb)
            q2_scr[...] = jnp.concatenate([jnp.where(lo, qb, zero), jnp.where(lo, zero, qb)], axis=0)
        if softmax:
            acc_w[...] = jnp.zeros(acc_w.shape, F32)

        def chunk(c, mx):
            rows = pl.ds(pl.multiple_of(c * kc, kc), kc)
            if softmax:
                e = jnp.exp2(st_prv[rows, :] - m_prev).astype(BF16)
            if scores:
                sn = lax.dot_general(k_ref[0, rows, :], q2_scr[...], (((1,), (1,)), ((), ())),
                                     preferred_element_type=F32)
                st_cur[rows, :] = sn
                mx = jnp.maximum(mx, jnp.max(sn, axis=0, keepdims=True))
            if softmax:
                acc_w[...] += jnp.dot(vt_ref[0, c], e, preferred_element_type=F32)
            return mx

        return lax.fori_loop(0, S // kc, chunk, jnp.full((1, 2 * tq), NEG_INF, F32), unroll=True)

    m = stage(0, None, st_a, None, None, None, softmax=False, fin=False)
    m = stage(1, m, st_b, st_a, acc_b, None, fin=False)

    def two_stages(j, m):
        m = stage(2 * j, m, st_a, st_b, acc_a, acc_b)
        return stage(2 * j + 1, m, st_b, st_a, acc_b, acc_a)

    m = lax.fori_loop(1, n_units // 2, two_stages, m)
    stage(n_units, m, None, st_b, acc_a, acc_b, scores=False)
    finalize(n_units - 1, acc_a)


def _mix_a(qa, ka, vat):
    B, _, S, _ = qa.shape
    tq = TQ_A
    _, nkc, vrows, kc = vat.shape
    return pl.pallas_call(
        _mixa_body,
        grid=(B,),
        in_specs=[
            pl.BlockSpec((1, N_PAIRS_A, S, LANES), lambda b: (b, 0, 0, 0)),
            pl.BlockSpec((1, S, AKV_WIDTH), lambda b: (b, 0, 0)),
            pl.BlockSpec((1, nkc, vrows, kc), lambda b: (b, 0, 0, 0)),
        ],
        out_specs=pl.BlockSpec((1, N_PAIRS_A, S, LANES), lambda b: (b, 0, 0, 0)),
        out_shape=jax.ShapeDtypeStruct((B, N_PAIRS_A, S, LANES), BF16),
        scratch_shapes=[pltpu.VMEM((S, 2 * tq), F32)] * 2 + [pltpu.VMEM((vrows, 2 * tq), F32)] * 2
        + [pltpu.VMEM((2 * tq, LANES), BF16)],
        compiler_params=pltpu.CompilerParams(
            dimension_semantics=("arbitrary",), vmem_limit_bytes=VMEM_LIMIT),
        name="mix_a",
    )(qa, ka, vat)


def _mixb_body(q_ref, *rest):
    nbr = len(DILATIONS)
    kv_refs = rest[:2 * nbr]
    bm_ref, o_ref, oacc, macc, dacc = rest[2 * nbr:]
    S = q_ref.shape[1]
    lane = lax.broadcasted_iota(jnp.int32, (1, LANES), 1)
    lo = lane < HEAD_DIM
    ones = jnp.ones((KW, LANES), BF16)
    order = sorted(range(nbr), key=lambda bi: -DILATIONS[bi])
    assert DILATIONS[order[-1]] == 1

    for slot, bi in enumerate(order):
        d = DILATIONS[bi]
        k_ref, v_ref = kv_refs[2 * bi], kv_refs[2 * bi + 1]
        L = S // d
        nc = L // QC
        shift = nc.bit_length() - 1
        last = slot == nbr - 1

        def body(it, carry, bi=bi, d=d, L=L, nc=nc, shift=shift, slot=slot, k_ref=k_ref, v_ref=v_ref, last=last):
            r = lax.shift_right_logical(it, shift)
            c = it & (nc - 1)
            j0 = c * QC
            ws = pl.multiple_of(jnp.clip(j0 - HALF, 0, L - KW), HALF)
            var = jnp.where(c == 0, 0, jnp.where(c == nc - 1, 2, 1))
            trow = pl.ds(pl.multiple_of(j0, QC), QC) if d == 1 else pl.ds(r + d * j0, QC, stride=d)
            qc = q_ref[0, trow, :].astype(BF16)
            kw = k_ref[0, r, pl.ds(ws, KW), :]
            vw = jnp.concatenate([v_ref[0, r, pl.ds(ws, KW), :], ones], axis=1)
            zero = jnp.zeros_like(qc)
            q2 = jnp.concatenate([jnp.where(lo, qc, zero), jnp.where(lo, zero, qc)], axis=0)
            s = lax.dot_general(q2, kw, (((1,), (1,)), ((), ())), preferred_element_type=F32)
            s = s + bm_ref[bi, var].reshape(2 * QC, KW)
            m = jnp.max(s, axis=-1, keepdims=True)
            e = jnp.exp2(s - m).astype(BF16)
            ox = jnp.dot(e, vw, preferred_element_type=F32)
            o = jnp.where(lo, ox[0:QC, 0:LANES], ox[QC:2 * QC, 0:LANES])
            den = jnp.where(lo, ox[0:QC, LANES:2 * LANES], ox[QC:2 * QC, LANES:2 * LANES])
            mb = jnp.where(lo, m[0:QC], m[QC:2 * QC])
            if not last:
                oacc[slot, trow, :] = o
                dacc[slot, trow, :] = den
                macc[slot, trow, :] = mb
            else:
                ms = [macc[sl, trow, :] for sl in range(nbr - 1)] + [mb]
                os_ = [oacc[sl, trow, :] for sl in range(nbr - 1)] + [o]
                ds_ = [dacc[sl, trow, :] for sl in range(nbr - 1)] + [den]
                mm = functools.reduce(jnp.maximum, ms)
                ws_ = [jnp.exp2(mi - mm) for mi in ms]
                num = functools.reduce(lambda a, b: a + b, [w * oi for w, oi in zip(ws_, os_)])
                dsum = functools.reduce(lambda a, b: a + b, [w * di for w, di in zip(ws_, ds_)])
                o_ref[0, trow, :] = (num / dsum).astype(BF16)
            return carry

        lax.fori_loop(0, d * nc, body, 0, unroll=True)


def _mix_b(qb, kvs, bmt):
    B, S, _ = qb.shape
    nbr = len(DILATIONS)
    assert all((S // d) % QC == 0 and S // d >= KW and ((S // d // QC) & (S // d // QC - 1)) == 0 for d in DILATIONS)
    kv_specs = []
    for d in DILATIONS:
        kv_specs.append(pl.BlockSpec((1, d, S // d, LANES), lambda p, b: (b, 0, 0, p)))
        kv_specs.append(pl.BlockSpec((1, d, S // d, LANES), lambda p, b: (b, 0, 0, N_PAIRS_B + p)))
    return pl.pallas_call(
        _mixb_body,
        grid=(N_PAIRS_B, B),
        in_specs=[pl.BlockSpec((1, S, LANES), lambda p, b: (b, 0, p))] + kv_specs + [
            pl.BlockSpec((nbr, N_VARIANTS, 2, QC, KW), lambda p, b: (0, 0, p, 0, 0)),
        ],
        out_specs=pl.BlockSpec((1, S, LANES), lambda p, b: (b, 0, p)),
        out_shape=jax.ShapeDtypeStruct((B, S, B_WIDTH), BF16),
        scratch_shapes=[pltpu.VMEM((nbr - 1, S, LANES), F32)] * 3,
        compiler_params=pltpu.CompilerParams(
            dimension_semantics=("arbitrary", "arbitrary"), vmem_limit_bytes=VMEM_LIMIT),
        name="mix_b",
    )(qb, *[kv for kv in kvs for _ in range(2)], bmt)


def _ffn_body(x_ref, xp_ref, xn_ref, oa_ref, oap_ref, oan_ref, ob_ref, obp_ref, obn_ref,
              wo_ref, g_ref, wup_ref, cw_ref, cb_ref, wd_ref, fn_ref, o_ref, o_scr, up_scr, y_scr, *, final):
    t = pl.program_id(1)
    nt = pl.num_programs(1)
    tm = x_ref.shape[1]
    rows = tm + 2 * HALO
    dff = wd_ref.shape[0]

    for i in range(N_PAIRS_A + 1):
        cols = slice(LANES * i, LANES * (i + 1)) if i < N_PAIRS_A else slice(A_WIDTH, A_WIDTH + B_WIDTH)
        o_scr[0:HALO, cols] = oap_ref[0, i] if i < N_PAIRS_A else obp_ref[0]
        o_scr[HALO:HALO + tm, cols] = oa_ref[0, i] if i < N_PAIRS_A else ob_ref[0]
        o_scr[HALO + tm:rows, cols] = oan_ref[0, i] if i < N_PAIRS_A else obn_ref[0]
    x_ext = jnp.concatenate([xp_ref[0], x_ref[0], xn_ref[0]], axis=0)
    x1 = x_ext + jnp.dot(o_scr[...], wo_ref[...], preferred_element_type=F32)
    rid = lax.broadcasted_iota(jnp.int32, (rows, 1), 0)
    inside = jnp.logical_and(jnp.logical_or(t > 0, rid >= HALO), jnp.logical_or(t < nt - 1, rid < HALO + tm))
    h = jnp.where(inside, _rms(x1, g_ref[...]), 0.0).astype(BF16)

    half = tm // 2
    nlt = FFN_CHUNK // LANES
    y = None
    for ci, c0 in enumerate(range(0, dff, FFN_CHUNK)):
        cw = min(FFN_CHUNK, dff - c0)

        def conv(off, gv):
            up = jnp.dot(h, wup_ref[:, off:off + cw], preferred_element_type=F32)
            tiles = []
            for lt in range(cw // LANES):
                buf = up_scr.at[(ci % 2) * 2 * nlt + gv * nlt + lt]
                buf[...] = up[:, LANES * lt:LANES * (lt + 1)]
                win = [buf[pl.ds(HALO - 1 + k, half, stride=2), :] for k in range(4)]
                prev = jnp.concatenate([win[0], win[1]], axis=0)
                mid = jnp.concatenate([win[1], win[2]], axis=0)
                nxt = jnp.concatenate([win[2], win[3]], axis=0)
                cols = slice(off + LANES * lt, off + LANES * (lt + 1))
                tiles.append(cb_ref[:, cols] + prev * cw_ref[0:1, cols] + mid * cw_ref[1:2, cols]
                             + nxt * cw_ref[2:3, cols])
            return jnp.concatenate(tiles, axis=1)

        ug = conv(c0, 0)
        uv = conv(dff + c0, 1)
        act = (ug * (1.0 / (1.0 + jnp.exp(-ug))) * uv).astype(BF16)
        part = jnp.dot(act, wd_ref[c0:c0 + cw, :], preferred_element_type=F32)
        y = part if y is None else y + part
    outs = []
    for lt in range(y.shape[1] // LANES):
        cols = slice(LANES * lt, LANES * (lt + 1))
        y_scr[lt, pl.ds(0, half, stride=2), :] = y[0:half, cols]
        y_scr[lt, pl.ds(1, half, stride=2), :] = y[half:tm, cols]
        outs.append(y_scr[lt])
    y = x1[HALO:HALO + tm] + jnp.concatenate(outs, axis=1)
    o_ref[0] = _rms(y, fn_ref[...]) if final else y


def _ffn(x, oa, ob, w_out, g, w_up, conv_w, conv_b, w_down, fn, final):
    B, S, D = x.shape
    tm = TM_FFN
    assert S % tm == 0 and tm % HALO == 0 and w_down.shape[0] % LANES == 0 and FFN_CHUNK % LANES == 0
    hb = tm // HALO
    nhb = S // HALO

    def prev_blk(t):
        return jnp.maximum(t * hb - 1, 0)

    def next_blk(t):
        return jnp.minimum((t + 1) * hb, nhb - 1)

    def resident(a):
        return pl.BlockSpec(a.shape, lambda b, t: (0,) * a.ndim, pipeline_mode=pl.Buffered(1))

    return pl.pallas_call(
        functools.partial(_ffn_body, final=final),
        grid=(B, S // tm),
        in_specs=[
            pl.BlockSpec((1, tm, D), lambda b, t: (b, t, 0)),
            pl.BlockSpec((1, HALO, D), lambda b, t: (b, prev_blk(t), 0)),
            pl.BlockSpec((1, HALO, D), lambda b, t: (b, next_blk(t), 0)),
            pl.BlockSpec((1, N_PAIRS_A, tm, LANES), lambda b, t: (b, 0, t, 0)),
            pl.BlockSpec((1, N_PAIRS_A, HALO, LANES), lambda b, t: (b, 0, prev_blk(t), 0)),
            pl.BlockSpec((1, N_PAIRS_A, HALO, LANES), lambda b, t: (b, 0, next_blk(t), 0)),
            pl.BlockSpec((1, tm, B_WIDTH), lambda b, t: (b, t, 0)),
            pl.BlockSpec((1, HALO, B_WIDTH), lambda b, t: (b, prev_blk(t), 0)),
            pl.BlockSpec((1, HALO, B_WIDTH), lambda b, t: (b, next_blk(t), 0)),
            resident(w_out), resident(g), resident(w_up), resident(conv_w), resident(conv_b),
            resident(w_down), resident(fn),
        ],
        out_specs=pl.BlockSpec((1, tm, D), lambda b, t: (b, t, 0)),
        out_shape=jax.ShapeDtypeStruct((B, S, D), F32),
        scratch_shapes=[pltpu.VMEM((tm + 2 * HALO, A_WIDTH + B_WIDTH), BF16),
                        pltpu.VMEM((4 * (FFN_CHUNK // LANES), tm + 2 * HALO, LANES), F32),
                        pltpu.VMEM((D // LANES, tm, LANES), F32)],
        compiler_params=pltpu.CompilerParams(
            dimension_semantics=("arbitrary", "arbitrary"), vmem_limit_bytes=VMEM_LIMIT),
        name="ffn",
    )(x, x, x, oa, oa, oa, ob, ob, ob, w_out, g, w_up, conv_w, conv_b, w_down, fn)


def _rope_tables(S):
    rows = S // GRID_W
    row = jnp.repeat(jnp.arange(rows), GRID_W).astype(F32)
    col = jnp.tile(jnp.arange(GRID_W), rows).astype(F32)
    n = HEAD_DIM // 4
    inv = ROPE_THETA ** (-jnp.arange(n, dtype=F32) / n)
    ang = jnp.concatenate([row[:, None] * inv, col[:, None] * inv], axis=-1)
    lane = jnp.arange(LANES)
    pair = (lane % HEAD_DIM) // 2
    sign = jnp.where(lane % 2 == 0, -1.0, 1.0).astype(F32)
    return jnp.cos(ang)[:, pair], jnp.sin(ang)[:, pair] * sign


def _qa_column_order():
    c = jnp.arange(A_WIDTH)
    tile, half, dim = c // LANES, (c % LANES) // HEAD_DIM, c % HEAD_DIM
    return (tile + (A_HEADS // A_KV_HEADS) * half) * HEAD_DIM + dim


def _trunk(x, p):
    depth = p["w_in"].shape[0]
    for l in range(depth):
        qa, ka, vat, qb, *kvs = _in_proj(x, p["attn_norm"][l], p["w_in"][l], p["gq"][l], p["gk"][l], p["cos"], p["sin"])
        oa = _mix_a(qa, ka, vat)
        ob = _mix_b(qb, kvs, p["bias"])
        x = _ffn(x, oa, ob, p["w_out"][l], p["ffn_norm"][l], p["w_up"][l], p["conv_w"][l], p["conv_b"][l],
                 p["w_down"][l], p["final_norm"], final=(l == depth - 1))
    return x


def kernel(x_prompt, x_sample, attn_norm, w_in, q_norm, k_norm, rel_bias, w_out, ffn_norm, w_up, conv_w, conv_b, w_down, final_norm):
    depth, d_model, _ = w_in.shape
    order = _qa_column_order()
    w_in_p = jnp.concatenate([w_in[:, :, order], w_in[:, :, A_WIDTH:]], axis=-1).astype(BF16)
    w_out_p = jnp.concatenate([w_out[:, order, :], w_out[:, A_WIDTH:, :]], axis=1).astype(BF16)
    params = {
        "attn_norm": attn_norm.reshape(depth, 1, d_model),
        "w_in": w_in_p,
        "gq": (jnp.tile(q_norm, (1, 2)) * Q_SCALE).reshape(depth, 1, LANES),
        "gk": jnp.tile(k_norm, (1, 2)).reshape(depth, 1, LANES),
        "w_out": w_out_p,
        "ffn_norm": ffn_norm.reshape(depth, 1, d_model),
        "w_up": w_up.astype(BF16),
        "conv_w": conv_w,
        "conv_b": conv_b.reshape(depth, 1, -1),
        "w_down": w_down.astype(BF16),
        "final_norm": final_norm.reshape(1, d_model),
        "bias": _bias_tiles(rel_bias),
    }
    outs = []
    for x in (x_prompt, x_sample):
        cos2, sin2 = _rope_tables(x.shape[1])
        outs.append(_trunk(x, dict(params, cos=cos2, sin=sin2)))
    return tuple(outs)
```

```python
import functools
import math

import jax
import jax.numpy as jnp
from jax import lax
from jax.experimental import pallas as pl
from jax.experimental.pallas import tpu as pltpu

F32 = jnp.float32
BF16 = jnp.bfloat16

HEAD_DIM = 64
A_HEADS = 8
A_KV_HEADS = 2
B_HEADS = 8
GRID_W = 64
ROPE_THETA = 10000.0
DILATED_BRANCHES = ((128, 1), (512, 4), (2048, 16))
N_BUCKETS = 32
REL_MAX_DIST = 1024
CONV_WIDTH = 3
EPS = 1e-6
NEG_INF = -1e30
ATTN_SCALE = HEAD_DIM ** -0.5
LOG2E = math.log2(math.e)
Q_SCALE = ATTN_SCALE * LOG2E

LANES = 128
A_WIDTH = A_HEADS * HEAD_DIM
AKV_WIDTH = A_KV_HEADS * HEAD_DIM
B_WIDTH = B_HEADS * HEAD_DIM
N_PAIRS_A = A_WIDTH // LANES
N_PAIRS_B = B_WIDTH // LANES

DILATIONS = tuple(d for _, d in DILATED_BRANCHES)
HALF = DILATED_BRANCHES[0][0] // (2 * DILATED_BRANCHES[0][1])
assert all(w // (2 * d) == HALF for w, d in DILATED_BRANCHES)
QC = 128
KW = QC + 2 * HALF
N_VARIANTS = 3

TM_PROJ = 512
TQ_A = 256
TM_FFN = 512
FFN_CHUNK = 1024
HALO = 16
ONES_ROWS = 16
VMEM_LIMIT = 56 * 1024 * 1024


def _rms(x, g):
    ms = jnp.mean(x * x, axis=-1, keepdims=True)
    return x * lax.rsqrt(ms + EPS) * g


def _bias_body(tab_ref, out_ref):
    br = pl.program_id(0)
    var = pl.program_id(1)
    dil = jnp.where(br == 0, DILATIONS[0], jnp.where(br == 1, DILATIONS[1], DILATIONS[2]))
    i = lax.broadcasted_iota(jnp.int32, (QC, KW), 0)
    jj = lax.broadcasted_iota(jnp.int32, (QC, KW), 1)
    delta = jj - var * HALF - i
    rel = delta * dil
    nb = N_BUCKETS // 2
    max_exact = nb // 2
    n = jnp.abs(rel)
    large = max_exact + (jnp.log(jnp.maximum(n, 1).astype(F32) / max_exact)
                         / math.log(REL_MAX_DIST / max_exact) * (nb - max_exact)).astype(jnp.int32)
    large = jnp.minimum(large, nb - 1)
    bucket = jnp.where(rel > 0, nb, 0) + jnp.where(n < max_exact, n, large)
    band = jnp.abs(delta) <= HALF
    for h in range(B_HEADS):
        acc = jnp.zeros((QC, KW), F32)
        for b in range(N_BUCKETS):
            acc = jnp.where(bucket == b, tab_ref[b, h], acc)
        out_ref[0, 0, h] = jnp.where(band, acc * LOG2E, NEG_INF)


def _bias_tiles(rel_bias):
    nbr = len(DILATIONS)
    return pl.pallas_call(
        _bias_body,
        grid=(nbr, N_VARIANTS),
        in_specs=[pl.BlockSpec(memory_space=pltpu.SMEM)],
        out_specs=pl.BlockSpec((1, 1, B_HEADS, QC, KW), lambda b, v: (b, v, 0, 0, 0)),
        out_shape=jax.ShapeDtypeStruct((nbr, N_VARIANTS, B_HEADS, QC, KW), F32),
        name="bias_tiles",
    )(rel_bias)


def _inproj_body(x_ref, g_ref, w_ref, gq_ref, gk_ref, cos_ref, sin_ref,
                 qa_ref, ka_ref, vat_ref, qb_ref, *rest):
    kv_refs, kv_scr, kv4_scr = rest[:-2], rest[-2], rest[-1]
    assert DILATIONS == (1, 4, 16)
    h = _rms(x_ref[0], g_ref[...]).astype(BF16)
    lane = lax.broadcasted_iota(jnp.int32, (1, LANES), 1)
    lo = lane < HEAD_DIM
    even = (lane & 1) == 0
    cos = cos_ref[...]
    sin = sin_ref[...]

    def norm_rope(blk, gain):
        sq = blk * blk
        s_lo = jnp.sum(jnp.where(lo, sq, 0.0), axis=-1, keepdims=True)
        s_hi = jnp.sum(jnp.where(lo, 0.0, sq), axis=-1, keepdims=True)
        r = jnp.where(lo, lax.rsqrt(s_lo * (1.0 / HEAD_DIM) + EPS), lax.rsqrt(s_hi * (1.0 / HEAD_DIM) + EPS))
        y = blk * r * gain
        partner = jnp.where(even, pltpu.roll(y, LANES - 1, 1), pltpu.roll(y, 1, 1))
        return y * cos + partner * sin

    na = A_WIDTH + 2 * AKV_WIDTH
    pa = jnp.dot(h, w_ref[:, 0:na], preferred_element_type=F32)
    for i in range(N_PAIRS_A):
        qa_ref[0, i] = norm_rope(pa[:, LANES * i:LANES * (i + 1)], gq_ref[...]).astype(BF16)
    ka_ref[0] = norm_rope(pa[:, A_WIDTH:A_WIDTH + AKV_WIDTH], gk_ref[...]).astype(BF16)
    vat_ref[0, 0, 0:AKV_WIDTH] = pa[:, A_WIDTH + AKV_WIDTH:na].T.astype(BF16)
    vat_ref[0, 0, AKV_WIDTH:AKV_WIDTH + ONES_ROWS] = jnp.ones((ONES_ROWS, x_ref.shape[1]), BF16)
    qb_ref[0] = jnp.dot(h, w_ref[:, na:na + B_WIDTH], preferred_element_type=F32) * Q_SCALE
    tm = x_ref.shape[1]
    step = 2 * LANES
    for c0 in range(0, 2 * B_WIDTH, step):
        pkv = jnp.dot(h, w_ref[:, na + B_WIDTH + c0:na + B_WIDTH + c0 + step], preferred_element_type=F32)
        for half in range(step // LANES):
            ct = c0 // LANES + half
            cols = slice(LANES * ct, LANES * (ct + 1))
            tile = pkv[:, LANES * half:LANES * (half + 1)]
            kv_scr[ct] = tile
            kv1_ref, kv4_ref, kv16_ref = kv_refs
            kv1_ref[0, 0, :, cols] = tile.astype(BF16)
            for r in range(4):
                g4 = kv_scr[ct, pl.ds(r, tm // 4, stride=4), :]
                kv4_ref[0, r, :, cols] = g4.astype(BF16)
                kv4_scr[ct, r * (tm // 4):(r + 1) * (tm // 4)] = g4
            for r in range(16):
                g16 = kv4_scr[ct, pl.ds((r % 4) * (tm // 4) + r // 4, tm // 16, stride=4), :]
                kv16_ref[0, r, :, cols] = g16.astype(BF16)


def _in_proj(x, g, w, gq, gk, cos2, sin2):
    B, S, D = x.shape
    tm = TM_PROJ
    nw = w.shape[1]
    return pl.pallas_call(
        _inproj_body,
        grid=(B, S // tm),
        in_specs=[
            pl.BlockSpec((1, tm, D), lambda b, t: (b, t, 0)),
            pl.BlockSpec((1, D), lambda b, t: (0, 0)),
            pl.BlockSpec((D, nw), lambda b, t: (0, 0)),
            pl.BlockSpec((1, LANES), lambda b, t: (0, 0)),
            pl.BlockSpec((1, LANES), lambda b, t: (0, 0)),
            pl.BlockSpec((tm, LANES), lambda b, t: (t, 0)),
            pl.BlockSpec((tm, LANES), lambda b, t: (t, 0)),
        ],
        out_specs=[
            pl.BlockSpec((1, N_PAIRS_A, tm, LANES), lambda b, t: (b, 0, t, 0)),
            pl.BlockSpec((1, tm, AKV_WIDTH), lambda b, t: (b, t, 0)),
            pl.BlockSpec((1, 1, AKV_WIDTH + ONES_ROWS, tm), lambda b, t: (b, t, 0, 0)),
            pl.BlockSpec((1, tm, B_WIDTH), lambda b, t: (b, t, 0)),
        ] + [pl.BlockSpec((1, d, tm // d, 2 * B_WIDTH), lambda b, t: (b, 0, t, 0)) for d in DILATIONS],
        out_shape=[
            jax.ShapeDtypeStruct((B, N_PAIRS_A, S, LANES), BF16),
            jax.ShapeDtypeStruct((B, S, AKV_WIDTH), BF16),
            jax.ShapeDtypeStruct((B, S // tm, AKV_WIDTH + ONES_ROWS, tm), BF16),
            jax.ShapeDtypeStruct((B, S, B_WIDTH), F32),
        ] + [jax.ShapeDtypeStruct((B, d, S // d, 2 * B_WIDTH), BF16) for d in DILATIONS],
        scratch_shapes=[pltpu.VMEM((2 * N_PAIRS_B, tm, LANES), F32)] * 2,
        compiler_params=pltpu.CompilerParams(
            dimension_semantics=("arbitrary", "arbitrary"), vmem_limit_bytes=VMEM_LIMIT),
        name="in_proj",
    )(x, g, w, gq, gk, cos2, sin2)


def _mixa_body(q_ref, k_ref, vt_ref, o_ref, st_a, st_b, acc_a, acc_b, q2_scr):
    S = k_ref.shape[1]
    tq = TQ_A
    kc = vt_ref.shape[3]
    n_units = (S // tq) * N_PAIRS_A
    assert n_units % 2 == 0 and n_units >= 4
    pair_shift = N_PAIRS_A.bit_length() - 1
    lane = lax.broadcasted_iota(jnp.int32, (1, LANES), 1)
    lo = lane < HEAD_DIM

    def finalize(u, acc):
        t = lax.shift_right_logical(u, pair_shift)
        i = u & (N_PAIRS_A - 1)
        ot = acc[...]
        ot = ot[0:AKV_WIDTH] / ot[AKV_WIDTH:AKV_WIDTH + 1]
        oblk = jnp.concatenate([ot[0:HEAD_DIM, 0:tq], ot[HEAD_DIM:2 * HEAD_DIM, tq:2 * tq]], axis=0)
        o_ref[0, i, pl.ds(pl.multiple_of(t * tq, tq), tq), :] = oblk.T.astype(BF16)

    def stage(s, m_prev, st_cur, st_prv, acc_w, acc_r, scores=True, softmax=True, fin=True):
        if fin:
            finalize(s - 2, acc_r)
        if scores:
            t = lax.shift_right_logical(s, pair_shift)
            i = s & (N_PAIRS_A - 1)
            qb = q_ref[0, i, pl.ds(pl.multiple_of(t * tq, tq), tq), :]
            zero = jnp.zeros_like(qb)
            q2_scr[...] = jnp.concatenate([jnp.where(lo, qb, zero), jnp.where(lo, zero, qb)], axis=0)
        if softmax:
            acc_w[...] = jnp.zeros(acc_w.shape, F32)

        def chunk(c, mx):
            rows = pl.ds(pl.multiple_of(c * kc, kc), kc)
            if softmax:
                e = jnp.exp2(st_prv[rows, :] - m_prev).astype(BF16)
            if scores:
                sn = lax.dot_general(k_ref[0, rows, :], q2_scr[...], (((1,), (1,)), ((), ())),
                                     preferred_element_type=F32)
                st_cur[rows, :] = sn
                mx = jnp.maximum(mx, jnp.max(sn, axis=0, keepdims=True))
            if softmax:
                acc_w[...] += jnp.dot(vt_ref[0, c], e, preferred_element_type=F32)
            return mx

        return lax.fori_loop(0, S // kc, chunk, jnp.full((1, 2 * tq), NEG_INF, F32), unroll=True)

    m = stage(0, None, st_a, None, None, None, softmax=False, fin=False)
    m = stage(1, m, st_b, st_a, acc_b, None, fin=False)

    def two_stages(j, m):
        m = stage(2 * j, m, st_a, st_b, acc_a, acc_b)
        return stage(2 * j + 1, m, st_b, st_a, acc_b, acc_a)

    m = lax.fori_loop(1, n_units // 2, two_stages, m)
    stage(n_units, m, None, st_b, acc_a, acc_b, scores=False)
    finalize(n_units - 1, acc_a)


def _mix_a(qa, ka, vat):
    B, _, S, _ = qa.shape
    tq = TQ_A
    _, nkc, vrows, kc = vat.shape
    return pl.pallas_call(
        _mixa_body,
        grid=(B,),
        in_specs=[
            pl.BlockSpec((1, N_PAIRS_A, S, LANES), lambda b: (b, 0, 0, 0)),
            pl.BlockSpec((1, S, AKV_WIDTH), lambda b: (b, 0, 0)),
            pl.BlockSpec((1, nkc, vrows, kc), lambda b: (b, 0, 0, 0)),
        ],
        out_specs=pl.BlockSpec((1, N_PAIRS_A, S, LANES), lambda b: (b, 0, 0, 0)),
        out_shape=jax.ShapeDtypeStruct((B, N_PAIRS_A, S, LANES), BF16),
        scratch_shapes=[pltpu.VMEM((S, 2 * tq), F32)] * 2 + [pltpu.VMEM((vrows, 2 * tq), F32)] * 2
        + [pltpu.VMEM((2 * tq, LANES), BF16)],
        compiler_params=pltpu.CompilerParams(
            dimension_semantics=("arbitrary",), vmem_limit_bytes=VMEM_LIMIT),
        name="mix_a",
    )(qa, ka, vat)


def _mixb_body(q_ref, *rest):
    nbr = len(DILATIONS)
    kv_refs = rest[:2 * nbr]
    bm_ref, o_ref, oacc, macc, dacc = rest[2 * nbr:]
    S = q_ref.shape[1]
    lane = lax.broadcasted_iota(jnp.int32, (1, LANES), 1)
    lo = lane < HEAD_DIM
    ones = jnp.ones((KW, LANES), BF16)
    order = sorted(range(nbr), key=lambda bi: -DILATIONS[bi])
    assert DILATIONS[order[-1]] == 1

    for slot, bi in enumerate(order):
        d = DILATIONS[bi]
        k_ref, v_ref = kv_refs[2 * bi], kv_refs[2 * bi + 1]
        L = S // d
        nc = L // QC
        shift = nc.bit_length() - 1
        last = slot == nbr - 1

        def body(it, carry, bi=bi, d=d, L=L, nc=nc, shift=shift, slot=slot, k_ref=k_ref, v_ref=v_ref, last=last):
            r = lax.shift_right_logical(it, shift)
            c = it & (nc - 1)
            j0 = c * QC
            ws = pl.multiple_of(jnp.clip(j0 - HALF, 0, L - KW), HALF)
            var = jnp.where(c == 0, 0, jnp.where(c == nc - 1, 2, 1))
            trow = pl.ds(pl.multiple_of(j0, QC), QC) if d == 1 else pl.ds(r + d * j0, QC, stride=d)
            qc = q_ref[0, trow, :].astype(BF16)
            kw = k_ref[0, r, pl.ds(ws, KW), :]
            vw = jnp.concatenate([v_ref[0, r, pl.ds(ws, KW), :], ones], axis=1)
            zero = jnp.zeros_like(qc)
            q2 = jnp.concatenate([jnp.where(lo, qc, zero), jnp.where(lo, zero, qc)], axis=0)
            s = lax.dot_general(q2, kw, (((1,), (1,)), ((), ())), preferred_element_type=F32)
            s = s + bm_ref[bi, var].reshape(2 * QC, KW)
            m = jnp.max(s, axis=-1, keepdims=True)
            e = jnp.exp2(s - m).astype(BF16)
            ox = jnp.dot(e, vw, preferred_element_type=F32)
            o = jnp.where(lo, ox[0:QC, 0:LANES], ox[QC:2 * QC, 0:LANES])
            den = jnp.where(lo, ox[0:QC, LANES:2 * LANES], ox[QC:2 * QC, LANES:2 * LANES])
            mb = jnp.where(lo, m[0:QC], m[QC:2 * QC])
            if not last:
                oacc[slot, trow, :] = o
                dacc[slot, trow, :] = den
                macc[slot, trow, :] = mb
            else:
                ms = [macc[sl, trow, :] for sl in range(nbr - 1)] + [mb]
                os_ = [oacc[sl, trow, :] for sl in range(nbr - 1)] + [o]
                ds_ = [dacc[sl, trow, :] for sl in range(nbr - 1)] + [den]
                mm = functools.reduce(jnp.maximum, ms)
                ws_ = [jnp.exp2(mi - mm) for mi in ms]
                num = functools.reduce(lambda a, b: a + b, [w * oi for w, oi in zip(ws_, os_)])
                dsum = functools.reduce(lambda a, b: a + b, [w * di for w, di in zip(ws_, ds_)])
                o_ref[0, trow, :] = (num / dsum).astype(BF16)
            return carry

        lax.fori_loop(0, d * nc, body, 0, unroll=True)


def _mix_b(qb, kvs, bmt):
    B, S, _ = qb.shape
    nbr = len(DILATIONS)
    assert all((S // d) % QC == 0 and S // d >= KW and ((S // d // QC) & (S // d // QC - 1)) == 0 for d in DILATIONS)
    kv_specs = []
    for d in DILATIONS:
        kv_specs.append(pl.BlockSpec((1, d, S // d, LANES), lambda p, b: (b, 0, 0, p)))
        kv_specs.append(pl.BlockSpec((1, d, S // d, LANES), lambda p, b: (b, 0, 0, N_PAIRS_B + p)))
    return pl.pallas_call(
        _mixb_body,
        grid=(N_PAIRS_B, B),
        in_specs=[pl.BlockSpec((1, S, LANES), lambda p, b: (b, 0, p))] + kv_specs + [
            pl.BlockSpec((nbr, N_VARIANTS, 2, QC, KW), lambda p, b: (0, 0, p, 0, 0)),
        ],
        out_specs=pl.BlockSpec((1, S, LANES), lambda p, b: (b, 0, p)),
        out_shape=jax.ShapeDtypeStruct((B, S, B_WIDTH), BF16),
        scratch_shapes=[pltpu.VMEM((nbr - 1, S, LANES), F32)] * 3,
        compiler_params=pltpu.CompilerParams(
            dimension_semantics=("arbitrary", "arbitrary"), vmem_limit_bytes=VMEM_LIMIT),
        name="mix_b",
    )(qb, *[kv for kv in kvs for _ in range(2)], bmt)


def _ffn_body(x_ref, xp_ref, xn_ref, oa_ref, oap_ref, oan_ref, ob_ref, obp_ref, obn_ref,
              wo_ref, g_ref, wup_ref, cw_ref, cb_ref, wd_ref, fn_ref, o_ref, o_scr, up_scr, y_scr, *, final):
    t = pl.program_id(1)
    nt = pl.num_programs(1)
    tm = x_ref.shape[1]
    rows = tm + 2 * HALO
    dff = wd_ref.shape[0]

    for i in range(N_PAIRS_A + 1):
        cols = slice(LANES * i, LANES * (i + 1)) if i < N_PAIRS_A else slice(A_WIDTH, A_WIDTH + B_WIDTH)
        o_scr[0:HALO, cols] = oap_ref[0, i] if i < N_PAIRS_A else obp_ref[0]
        o_scr[HALO:HALO + tm, cols] = oa_ref[0, i] if i < N_PAIRS_A else ob_ref[0]
        o_scr[HALO + tm:rows, cols] = oan_ref[0, i] if i < N_PAIRS_A else obn_ref[0]
    x_ext = jnp.concatenate([xp_ref[0], x_ref[0], xn_ref[0]], axis=0)
    x1 = x_ext + jnp.dot(o_scr[...], wo_ref[...], preferred_element_type=F32)
    rid = lax.broadcasted_iota(jnp.int32, (rows, 1), 0)
    inside = jnp.logical_and(jnp.logical_or(t > 0, rid >= HALO), jnp.logical_or(t < nt - 1, rid < HALO + tm))
    h = jnp.where(inside, _rms(x1, g_ref[...]), 0.0).astype(BF16)

    half = tm // 2
    nlt = FFN_CHUNK // LANES
    y = None
    for ci, c0 in enumerate(range(0, dff, FFN_CHUNK)):
        cw = min(FFN_CHUNK, dff - c0)

        def conv(off, gv):
            up = jnp.dot(h, wup_ref[:, off:off + cw], preferred_element_type=F32)
            tiles = []
            for lt in range(cw // LANES):
                buf = up_scr.at[(ci % 2) * 2 * nlt + gv * nlt + lt]
                buf[...] = up[:, LANES * lt:LANES * (lt + 1)]
                win = [buf[pl.ds(HALO - 1 + k, half, stride=2), :] for k in range(4)]
                prev = jnp.concatenate([win[0], win[1]], axis=0)
                mid = jnp.concatenate([win[1], win[2]], axis=0)
                nxt = jnp.concatenate([win[2], win[3]], axis=0)
                cols = slice(off + LANES * lt, off + LANES * (lt + 1))
                tiles.append(cb_ref[:, cols] + prev * cw_ref[0:1, cols] + mid * cw_ref[1:2, cols]
                             + nxt * cw_ref[2:3, cols])
            return jnp.concatenate(tiles, axis=1)

        ug = conv(c0, 0)
        uv = conv(dff + c0, 1)
        act = (ug * (1.0 / (1.0 + jnp.exp(-ug))) * uv).astype(BF16)
        part = jnp.dot(act, wd_ref[c0:c0 + cw, :], preferred_element_type=F32)
        y = part if y is None else y + part
    outs = []
    for lt in range(y.shape[1] // LANES):
        cols = slice(LANES * lt, LANES * (lt + 1))
        y_scr[lt, pl.ds(0, half, stride=2), :] = y[0:half, cols]
        y_scr[lt, pl.ds(1, half, stride=2), :] = y[half:tm, cols]
        outs.append(y_scr[lt])
    y = x1[HALO:HALO + tm] + jnp.concatenate(outs, axis=1)
    o_ref[0] = _rms(y, fn_ref[...]) if final else y


def _ffn(x, oa, ob, w_out, g, w_up, conv_w, conv_b, w_down, fn, final):
    B, S, D = x.shape
    tm = TM_FFN
    assert S % tm == 0 and tm % HALO == 0 and w_down.shape[0] % LANES == 0 and FFN_CHUNK % LANES == 0
    hb = tm // HALO
    nhb = S // HALO

    def prev_blk(t):
        return jnp.maximum(t * hb - 1, 0)

    def next_blk(t):
        return jnp.minimum((t + 1) * hb, nhb - 1)

    def resident(a):
        return pl.BlockSpec(a.shape, lambda b, t: (0,) * a.ndim, pipeline_mode=pl.Buffered(1))

    return pl.pallas_call(
        functools.partial(_ffn_body, final=final),
        grid=(B, S // tm),
        in_specs=[
            pl.BlockSpec((1, tm, D), lambda b, t: (b, t, 0)),
            pl.BlockSpec((1, HALO, D), lambda b, t: (b, prev_blk(t), 0)),
            pl.BlockSpec((1, HALO, D), lambda b, t: (b, next_blk(t), 0)),
            pl.BlockSpec((1, N_PAIRS_A, tm, LANES), lambda b, t: (b, 0, t, 0)),
            pl.BlockSpec((1, N_PAIRS_A, HALO, LANES), lambda b, t: (b, 0, prev_blk(t), 0)),
            pl.BlockSpec((1, N_PAIRS_A, HALO, LANES), lambda b, t: (b, 0, next_blk(t), 0)),
            pl.BlockSpec((1, tm, B_WIDTH), lambda b, t: (b, t, 0)),
            pl.BlockSpec((1, HALO, B_WIDTH), lambda b, t: (b, prev_blk(t), 0)),
            pl.BlockSpec((1, HALO, B_WIDTH), lambda b, t: (b, next_blk(t), 0)),
            resident(w_out), resident(g), resident(w_up), resident(conv_w), resident(conv_b),
            resident(w_down), resident(fn),
        ],
        out_specs=pl.BlockSpec((1, tm, D), lambda b, t: (b, t, 0)),
        out_shape=jax.ShapeDtypeStruct((B, S, D), F32),
        scratch_shapes=[pltpu.VMEM((tm + 2 * HALO, A_WIDTH + B_WIDTH), BF16),
                        pltpu.VMEM((4 * (FFN_CHUNK // LANES), tm + 2 * HALO, LANES), F32),
                        pltpu.VMEM((D // LANES, tm, LANES), F32)],
        compiler_params=pltpu.CompilerParams(
            dimension_semantics=("arbitrary", "arbitrary"), vmem_limit_bytes=VMEM_LIMIT),
        name="ffn",
    )(x, x, x, oa, oa, oa, ob, ob, ob, w_out, g, w_up, conv_w, conv_b, w_down, fn)


def _rope_tables(S):
    rows = S // GRID_W
    row = jnp.repeat(jnp.arange(rows), GRID_W).astype(F32)
    col = jnp.tile(jnp.arange(GRID_W), rows).astype(F32)
    n = HEAD_DIM // 4
    inv = ROPE_THETA ** (-jnp.arange(n, dtype=F32) / n)
    ang = jnp.concatenate([row[:, None] * inv, col[:, None] * inv], axis=-1)
    lane = jnp.arange(LANES)
    pair = (lane % HEAD_DIM) // 2
    sign = jnp.where(lane % 2 == 0, -1.0, 1.0).astype(F32)
    return jnp.cos(ang)[:, pair], jnp.sin(ang)[:, pair] * sign


def _qa_column_order():
    c = jnp.arange(A_WIDTH)
    tile, half, dim = c // LANES, (c % LANES) // HEAD_DIM, c % HEAD_DIM
    return (tile + (A_HEADS // A_KV_HEADS) * half) * HEAD_DIM + dim


def _trunk(x, p):
    depth = p["w_in"].shape[0]
    for l in range(depth):
        qa, ka, vat, qb, *kvs = _in_proj(x, p["attn_norm"][l], p["w_in"][l], p["gq"][l], p["gk"][l], p["cos"], p["sin"])
        oa = _mix_a(qa, ka, vat)
        ob = _mix_b(qb, kvs, p["bias"])
        x = _ffn(x, oa, ob, p["w_out"][l], p["ffn_norm"][l], p["w_up"][l], p["conv_w"][l], p["conv_b"][l],
                 p["w_down"][l], p["final_norm"], final=(l == depth - 1))
    return x


def kernel(x_prompt, x_sample, attn_norm, w_in, q_norm, k_norm, rel_bias, w_out, ffn_norm, w_up, conv_w, conv_b, w_down, final_norm):
    depth, d_model, _ = w_in.shape
    order = _qa_column_order()
    w_in_p = jnp.concatenate([w_in[:, :, order], w_in[:, :, A_WIDTH:]], axis=-1).astype(BF16)
    w_out_p = jnp.concatenate([w_out[:, order, :], w_out[:, A_WIDTH:, :]], axis=1).astype(BF16)
    params = {
        "attn_norm": attn_norm.reshape(depth, 1, d_model),
        "w_in": w_in_p,
        "gq": (jnp.tile(q_norm, (1, 2)) * Q_SCALE).reshape(depth, 1, LANES),
        "gk": jnp.tile(k_norm, (1, 2)).reshape(depth, 1, LANES),
        "w_out": w_out_p,
        "ffn_norm": ffn_norm.reshape(depth, 1, d_model),
        "w_up": w_up.astype(BF16),
        "conv_w": conv_w,
        "conv_b": conv_b.reshape(depth, 1, -1),
        "w_down": w_down.astype(BF16),
        "final_norm": final_norm.reshape(1, d_model),
        "bias": _bias_tiles(rel_bias),
    }
    outs = []
    for x in (x_prompt, x_sample):
        cos2, sin2 = _rope_tables(x.shape[1])
        outs.append(_trunk(x, dict(params, cos=cos2, sin=sin2)))
    return tuple(outs)
```

```python
import functools
import math

import jax
import jax.numpy as jnp
from jax import lax
from jax.experimental import pallas as pl
from jax.experimental.pallas import tpu as pltpu

F32 = jnp.float32
BF16 = jnp.bfloat16

HEAD_DIM = 64
A_HEADS = 8
A_KV_HEADS = 2
B_HEADS = 8
GRID_W = 64
ROPE_THETA = 10000.0
DILATED_BRANCHES = ((128, 1), (512, 4), (2048, 16))
N_BUCKETS = 32
REL_MAX_DIST = 1024
CONV_WIDTH = 3
EPS = 1e-6
NEG_INF = -1e30
ATTN_SCALE = HEAD_DIM ** -0.5
LOG2E = math.log2(math.e)
Q_SCALE = ATTN_SCALE * LOG2E

LANES = 128
A_WIDTH = A_HEADS * HEAD_DIM
AKV_WIDTH = A_KV_HEADS * HEAD_DIM
B_WIDTH = B_HEADS * HEAD_DIM
N_PAIRS_A = A_WIDTH // LANES
N_PAIRS_B = B_WIDTH // LANES

DILATIONS = tuple(d for _, d in DILATED_BRANCHES)
HALF = DILATED_BRANCHES[0][0] // (2 * DILATED_BRANCHES[0][1])
assert all(w // (2 * d) == HALF for w, d in DILATED_BRANCHES)
QC = 128
KW = QC + 2 * HALF
N_VARIANTS = 3

TM_PROJ = 512
TQ_A = 256
TM_FFN = 512
FFN_CHUNK = 1024
HALO = 16
ONES_ROWS = 16
VMEM_LIMIT = 56 * 1024 * 1024


def _rms(x, g):
    ms = jnp.mean(x * x, axis=-1, keepdims=True)
    return x * lax.rsqrt(ms + EPS) * g


def _bias_body(tab_ref, out_ref):
    br = pl.program_id(0)
    var = pl.program_id(1)
    dil = jnp.where(br == 0, DILATIONS[0], jnp.where(br == 1, DILATIONS[1], DILATIONS[2]))
    i = lax.broadcasted_iota(jnp.int32, (QC, KW), 0)
    jj = lax.broadcasted_iota(jnp.int32, (QC, KW), 1)
    delta = jj - var * HALF - i
    rel = delta * dil
    nb = N_BUCKETS // 2
    max_exact = nb // 2
    n = jnp.abs(rel)
    large = max_exact + (jnp.log(jnp.maximum(n, 1).astype(F32) / max_exact)
                         / math.log(REL_MAX_DIST / max_exact) * (nb - max_exact)).astype(jnp.int32)
    large = jnp.minimum(large, nb - 1)
    bucket = jnp.where(rel > 0, nb, 0) + jnp.where(n < max_exact, n, large)
    band = jnp.abs(delta) <= HALF
    for h in range(B_HEADS):
        acc = jnp.zeros((QC, KW), F32)
        for b in range(N_BUCKETS):
            acc = jnp.where(bucket == b, tab_ref[b, h], acc)
        out_ref[0, 0, h] = jnp.where(band, acc * LOG2E, NEG_INF)


def _bias_tiles(rel_bias):
    nbr = len(DILATIONS)
    return pl.pallas_call(
        _bias_body,
        grid=(nbr, N_VARIANTS),
        in_specs=[pl.BlockSpec(memory_space=pltpu.SMEM)],
        out_specs=pl.BlockSpec((1, 1, B_HEADS, QC, KW), lambda b, v: (b, v, 0, 0, 0)),
        out_shape=jax.ShapeDtypeStruct((nbr, N_VARIANTS, B_HEADS, QC, KW), F32),
        name="bias_tiles",
    )(rel_bias)


def _inproj_body(x_ref, g_ref, w_ref, gq_ref, gk_ref, cos_ref, sin_ref,
                 qa_ref, ka_ref, vat_ref, qb_ref, *rest):
    kv_refs, kv_scr, kv4_scr = rest[:-2], rest[-2], rest[-1]
    assert DILATIONS == (1, 4, 16)
    h = _rms(x_ref[0], g_ref[...]).astype(BF16)
    lane = lax.broadcasted_iota(jnp.int32, (1, LANES), 1)
    lo = lane < HEAD_DIM
    even = (lane & 1) == 0
    cos = cos_ref[...]
    sin = sin_ref[...]

    def norm_rope(blk, gain):
        sq = blk * blk
        s_lo = jnp.sum(jnp.where(lo, sq, 0.0), axis=-1, keepdims=True)
        s_hi = jnp.sum(jnp.where(lo, 0.0, sq), axis=-1, keepdims=True)
        r = jnp.where(lo, lax.rsqrt(s_lo * (1.0 / HEAD_DIM) + EPS), lax.rsqrt(s_hi * (1.0 / HEAD_DIM) + EPS))
        y = blk * r * gain
        partner = jnp.where(even, pltpu.roll(y, LANES - 1, 1), pltpu.roll(y, 1, 1))
        return y * cos + partner * sin

    na = A_WIDTH + 2 * AKV_WIDTH
    pa = jnp.dot(h, w_ref[:, 0:na], preferred_element_type=F32)
    for i in range(N_PAIRS_A):
        qa_ref[0, i] = norm_rope(pa[:, LANES * i:LANES * (i + 1)], gq_ref[...]).astype(BF16)
    ka_ref[0] = norm_rope(pa[:, A_WIDTH:A_WIDTH + AKV_WIDTH], gk_ref[...]).astype(BF16)
    vt = pa[:, A_WIDTH + AKV_WIDTH:na].T.astype(BF16)
    for kv in range(A_KV_HEADS):
        vat_ref[0, 0, kv, 0:HEAD_DIM] = vt[HEAD_DIM * kv:HEAD_DIM * (kv + 1)]
        vat_ref[0, 0, kv, HEAD_DIM:HEAD_DIM + ONES_ROWS] = jnp.ones((ONES_ROWS, x_ref.shape[1]), BF16)
    qb_ref[0] = jnp.dot(h, w_ref[:, na:na + B_WIDTH], preferred_element_type=F32) * Q_SCALE
    tm = x_ref.shape[1]
    step = 2 * LANES
    for c0 in range(0, 2 * B_WIDTH, step):
        pkv = jnp.dot(h, w_ref[:, na + B_WIDTH + c0:na + B_WIDTH + c0 + step], preferred_element_type=F32)
        for half in range(step // LANES):
            ct = c0 // LANES + half
            cols = slice(LANES * ct, LANES * (ct + 1))
            tile = pkv[:, LANES * half:LANES * (half + 1)]
            kv_scr[ct] = tile
            kv1_ref, kv4_ref, kv16_ref = kv_refs
            kv1_ref[0, 0, :, cols] = tile.astype(BF16)
            for r in range(4):
                g4 = kv_scr[ct, pl.ds(r, tm // 4, stride=4), :]
                kv4_ref[0, r, :, cols] = g4.astype(BF16)
                kv4_scr[ct, r * (tm // 4):(r + 1) * (tm // 4)] = g4
            for r in range(16):
                g16 = kv4_scr[ct, pl.ds((r % 4) * (tm // 4) + r // 4, tm // 16, stride=4), :]
                kv16_ref[0, r, :, cols] = g16.astype(BF16)


def _in_proj(x, g, w, gq, gk, cos2, sin2):
    B, S, D = x.shape
    tm = TM_PROJ
    nw = w.shape[1]
    return pl.pallas_call(
        _inproj_body,
        grid=(B, S // tm),
        in_specs=[
            pl.BlockSpec((1, tm, D), lambda b, t: (b, t, 0)),
            pl.BlockSpec((1, D), lambda b, t: (0, 0)),
            pl.BlockSpec((D, nw), lambda b, t: (0, 0)),
            pl.BlockSpec((1, LANES), lambda b, t: (0, 0)),
            pl.BlockSpec((1, LANES), lambda b, t: (0, 0)),
            pl.BlockSpec((tm, LANES), lambda b, t: (t, 0)),
            pl.BlockSpec((tm, LANES), lambda b, t: (t, 0)),
        ],
        out_specs=[
            pl.BlockSpec((1, N_PAIRS_A, tm, LANES), lambda b, t: (b, 0, t, 0)),
            pl.BlockSpec((1, tm, AKV_WIDTH), lambda b, t: (b, t, 0)),
            pl.BlockSpec((1, 1, A_KV_HEADS, HEAD_DIM + ONES_ROWS, tm), lambda b, t: (b, t, 0, 0, 0)),
            pl.BlockSpec((1, tm, B_WIDTH), lambda b, t: (b, t, 0)),
        ] + [pl.BlockSpec((1, d, tm // d, 2 * B_WIDTH), lambda b, t: (b, 0, t, 0)) for d in DILATIONS],
        out_shape=[
            jax.ShapeDtypeStruct((B, N_PAIRS_A, S, LANES), BF16),
            jax.ShapeDtypeStruct((B, S, AKV_WIDTH), BF16),
            jax.ShapeDtypeStruct((B, S // tm, A_KV_HEADS, HEAD_DIM + ONES_ROWS, tm), BF16),
            jax.ShapeDtypeStruct((B, S, B_WIDTH), F32),
        ] + [jax.ShapeDtypeStruct((B, d, S // d, 2 * B_WIDTH), BF16) for d in DILATIONS],
        scratch_shapes=[pltpu.VMEM((2 * N_PAIRS_B, tm, LANES), F32)] * 2,
        compiler_params=pltpu.CompilerParams(
            dimension_semantics=("arbitrary", "arbitrary"), vmem_limit_bytes=VMEM_LIMIT),
        name="in_proj",
    )(x, g, w, gq, gk, cos2, sin2)


def _mixa_body(q_ref, k_ref, vt_ref, o_ref, st_a, st_b, acc_a, acc_b, q2_scr):
    S = k_ref.shape[1]
    tq = TQ_A
    kc = vt_ref.shape[4]
    n_units = (S // tq) * N_PAIRS_A
    assert n_units % 2 == 0 and n_units >= 4
    pair_shift = N_PAIRS_A.bit_length() - 1
    lane = lax.broadcasted_iota(jnp.int32, (1, LANES), 1)
    lo = lane < HEAD_DIM

    def finalize(u, acc):
        t = lax.shift_right_logical(u, pair_shift)
        i = u & (N_PAIRS_A - 1)
        halves = []
        for kv in range(A_KV_HEADS):
            ot = acc[kv]
            halves.append(ot[0:HEAD_DIM] / ot[HEAD_DIM:HEAD_DIM + 1])
        oblk = jnp.concatenate(halves, axis=0)
        o_ref[0, i, pl.ds(pl.multiple_of(t * tq, tq), tq), :] = oblk.T.astype(BF16)

    def stage(s, m_prev, st_cur, st_prv, acc_w, acc_r, scores=True, softmax=True, fin=True):
        if fin:
            finalize(s - 2, acc_r)
        if scores:
            t = lax.shift_right_logical(s, pair_shift)
            i = s & (N_PAIRS_A - 1)
            qb = q_ref[0, i, pl.ds(pl.multiple_of(t * tq, tq), tq), :]
            zero = jnp.zeros_like(qb)
            q2_scr[...] = jnp.concatenate([jnp.where(lo, qb, zero), jnp.where(lo, zero, qb)], axis=0)
        if softmax:
            acc_w[...] = jnp.zeros(acc_w.shape, F32)

        def chunk(c, mx):
            rows = pl.ds(pl.multiple_of(c * kc, kc), kc)
            if softmax:
                e = jnp.exp2(st_prv[rows, :] - m_prev).astype(BF16)
            if scores:
                sn = lax.dot_general(k_ref[0, rows, :], q2_scr[...], (((1,), (1,)), ((), ())),
                                     preferred_element_type=F32)
                st_cur[rows, :] = sn
                mx = jnp.maximum(mx, jnp.max(sn, axis=0, keepdims=True))
            if softmax:
                for kv in range(A_KV_HEADS):
                    acc_w[kv] += jnp.dot(vt_ref[0, c, kv], e[:, tq * kv:tq * (kv + 1)], preferred_element_type=F32)
            return mx

        return lax.fori_loop(0, S // kc, chunk, jnp.full((1, 2 * tq), NEG_INF, F32), unroll=True)

    m = stage(0, None, st_a, None, None, None, softmax=False, fin=False)
    m = stage(1, m, st_b, st_a, acc_b, None, fin=False)

    def two_stages(j, m):
        m = stage(2 * j, m, st_a, st_b, acc_a, acc_b)
        return stage(2 * j + 1, m, st_b, st_a, acc_b, acc_a)

    m = lax.fori_loop(1, n_units // 2, two_stages, m)
    stage(n_units, m, None, st_b, acc_a, acc_b, scores=False)
    finalize(n_units - 1, acc_a)


def _mix_a(qa, ka, vat):
    B, _, S, _ = qa.shape
    tq = TQ_A
    _, nkc, _, vrows, kc = vat.shape
    return pl.pallas_call(
        _mixa_body,
        grid=(B,),
        in_specs=[
            pl.BlockSpec((1, N_PAIRS_A, S, LANES), lambda b: (b, 0, 0, 0)),
            pl.BlockSpec((1, S, AKV_WIDTH), lambda b: (b, 0, 0)),
            pl.BlockSpec((1, nkc, A_KV_HEADS, vrows, kc), lambda b: (b, 0, 0, 0, 0)),
        ],
        out_specs=pl.BlockSpec((1, N_PAIRS_A, S, LANES), lambda b: (b, 0, 0, 0)),
        out_shape=jax.ShapeDtypeStruct((B, N_PAIRS_A, S, LANES), BF16),
        scratch_shapes=[pltpu.VMEM((S, 2 * tq), F32)] * 2 + [pltpu.VMEM((A_KV_HEADS, vrows, tq), F32)] * 2
        + [pltpu.VMEM((2 * tq, LANES), BF16)],
        compiler_params=pltpu.CompilerParams(
            dimension_semantics=("arbitrary",), vmem_limit_bytes=VMEM_LIMIT),
        name="mix_a",
    )(qa, ka, vat)


def _mixb_body(q_ref, *rest):
    nbr = len(DILATIONS)
    kv_refs = rest[:2 * nbr]
    bm_ref, o_ref, oacc, macc, dacc = rest[2 * nbr:]
    S = q_ref.shape[1]
    lane = lax.broadcasted_iota(jnp.int32, (1, LANES), 1)
    lo = lane < HEAD_DIM
    ones = jnp.ones((KW, LANES), BF16)
    order = sorted(range(nbr), key=lambda bi: -DILATIONS[bi])
    assert DILATIONS[order[-1]] == 1

    for slot, bi in enumerate(order):
        d = DILATIONS[bi]
        k_ref, v_ref = kv_refs[2 * bi], kv_refs[2 * bi + 1]
        L = S // d
        nc = L // QC
        shift = nc.bit_length() - 1
        last = slot == nbr - 1

        def body(it, carry, bi=bi, d=d, L=L, nc=nc, shift=shift, slot=slot, k_ref=k_ref, v_ref=v_ref, last=last):
            r = lax.shift_right_logical(it, shift)
            c = it & (nc - 1)
            j0 = c * QC
            ws = pl.multiple_of(jnp.clip(j0 - HALF, 0, L - KW), HALF)
            var = jnp.where(c == 0, 0, jnp.where(c == nc - 1, 2, 1))
            trow = pl.ds(pl.multiple_of(j0, QC), QC) if d == 1 else pl.ds(r + d * j0, QC, stride=d)
            qc = q_ref[0, trow, :].astype(BF16)
            kw = k_ref[0, r, pl.ds(ws, KW), :]
            vw = jnp.concatenate([v_ref[0, r, pl.ds(ws, KW), :], ones], axis=1)
            zero = jnp.zeros_like(qc)
            q2 = jnp.concatenate([jnp.where(lo, qc, zero), jnp.where(lo, zero, qc)], axis=0)
            s = lax.dot_general(q2, kw, (((1,), (1,)), ((), ())), preferred_element_type=F32)
            s = s + bm_ref[bi, var].reshape(2 * QC, KW)
            m = jnp.max(s, axis=-1, keepdims=True)
            e = jnp.exp2(s - m).astype(BF16)
            ox = jnp.dot(e, vw, preferred_element_type=F32)
            o = jnp.where(lo, ox[0:QC, 0:LANES], ox[QC:2 * QC, 0:LANES])
            den = jnp.where(lo, ox[0:QC, LANES:2 * LANES], ox[QC:2 * QC, LANES:2 * LANES])
            mb = jnp.where(lo, m[0:QC], m[QC:2 * QC])
            if not last:
                oacc[slot, trow, :] = o
                dacc[slot, trow, :] = den
                macc[slot, trow, :] = mb
            else:
                ms = [macc[sl, trow, :] for sl in range(nbr - 1)] + [mb]
                os_ = [oacc[sl, trow, :] for sl in range(nbr - 1)] + [o]
                ds_ = [dacc[sl, trow, :] for sl in range(nbr - 1)] + [den]
                mm = functools.reduce(jnp.maximum, ms)
                ws_ = [jnp.exp2(mi - mm) for mi in ms]
                num = functools.reduce(lambda a, b: a + b, [w * oi for w, oi in zip(ws_, os_)])
                dsum = functools.reduce(lambda a, b: a + b, [w * di for w, di in zip(ws_, ds_)])
                o_ref[0, trow, :] = (num / dsum).astype(BF16)
            return carry

        lax.fori_loop(0, d * nc, body, 0, unroll=True)


def _mix_b(qb, kvs, bmt):
    B, S, _ = qb.shape
    nbr = len(DILATIONS)
    assert all((S // d) % QC == 0 and S // d >= KW and ((S // d // QC) & (S // d // QC - 1)) == 0 for d in DILATIONS)
    kv_specs = []
    for d in DILATIONS:
        kv_specs.append(pl.BlockSpec((1, d, S // d, LANES), lambda p, b: (b, 0, 0, p)))
        kv_specs.append(pl.BlockSpec((1, d, S // d, LANES), lambda p, b: (b, 0, 0, N_PAIRS_B + p)))
    return pl.pallas_call(
        _mixb_body,
        grid=(N_PAIRS_B, B),
        in_specs=[pl.BlockSpec((1, S, LANES), lambda p, b: (b, 0, p))] + kv_specs + [
            pl.BlockSpec((nbr, N_VARIANTS, 2, QC, KW), lambda p, b: (0, 0, p, 0, 0)),
        ],
        out_specs=pl.BlockSpec((1, S, LANES), lambda p, b: (b, 0, p)),
        out_shape=jax.ShapeDtypeStruct((B, S, B_WIDTH), BF16),
        scratch_shapes=[pltpu.VMEM((nbr - 1, S, LANES), F32)] * 3,
        compiler_params=pltpu.CompilerParams(
            dimension_semantics=("arbitrary", "arbitrary"), vmem_limit_bytes=VMEM_LIMIT),
        name="mix_b",
    )(qb, *[kv for kv in kvs for _ in range(2)], bmt)


def _ffn_body(x_ref, xp_ref, xn_ref, oa_ref, oap_ref, oan_ref, ob_ref, obp_ref, obn_ref,
              wo_ref, g_ref, wup_ref, cw_ref, cb_ref, wd_ref, fn_ref, o_ref, o_scr, up_scr, y_scr, *, final):
    t = pl.program_id(1)
    nt = pl.num_programs(1)
    tm = x_ref.shape[1]
    rows = tm + 2 * HALO
    dff = wd_ref.shape[0]

    for i in range(N_PAIRS_A + 1):
        cols = slice(LANES * i, LANES * (i + 1)) if i < N_PAIRS_A else slice(A_WIDTH, A_WIDTH + B_WIDTH)
        o_scr[0:HALO, cols] = oap_ref[0, i] if i < N_PAIRS_A else obp_ref[0]
        o_scr[HALO:HALO + tm, cols] = oa_ref[0, i] if i < N_PAIRS_A else ob_ref[0]
        o_scr[HALO + tm:rows, cols] = oan_ref[0, i] if i < N_PAIRS_A else obn_ref[0]
    x_ext = jnp.concatenate([xp_ref[0], x_ref[0], xn_ref[0]], axis=0)
    x1 = x_ext + jnp.dot(o_scr[...], wo_ref[...], preferred_element_type=F32)
    rid = lax.broadcasted_iota(jnp.int32, (rows, 1), 0)
    inside = jnp.logical_and(jnp.logical_or(t > 0, rid >= HALO), jnp.logical_or(t < nt - 1, rid < HALO + tm))
    h = jnp.where(inside, _rms(x1, g_ref[...]), 0.0).astype(BF16)

    half = tm // 2
    nlt = FFN_CHUNK // LANES
    y = None
    for ci, c0 in enumerate(range(0, dff, FFN_CHUNK)):
        cw = min(FFN_CHUNK, dff - c0)

        def conv(off, gv):
            up = jnp.dot(h, wup_ref[:, off:off + cw], preferred_element_type=F32)
            tiles = []
            for lt in range(cw // LANES):
                buf = up_scr.at[(ci % 2) * 2 * nlt + gv * nlt + lt]
                buf[...] = up[:, LANES * lt:LANES * (lt + 1)]
                win = [buf[pl.ds(HALO - 1 + k, half, stride=2), :] for k in range(4)]
                prev = jnp.concatenate([win[0], win[1]], axis=0)
                mid = jnp.concatenate([win[1], win[2]], axis=0)
                nxt = jnp.concatenate([win[2], win[3]], axis=0)
                cols = slice(off + LANES * lt, off + LANES * (lt + 1))
                tiles.append(cb_ref[:, cols] + prev * cw_ref[0:1, cols] + mid * cw_ref[1:2, cols]
                             + nxt * cw_ref[2:3, cols])
            return jnp.concatenate(tiles, axis=1)

        ug = conv(c0, 0)
        uv = conv(dff + c0, 1)
        act = (ug * (1.0 / (1.0 + jnp.exp(-ug))) * uv).astype(BF16)
        part = jnp.dot(act, wd_ref[c0:c0 + cw, :], preferred_element_type=F32)
        y = part if y is None else y + part
    outs = []
    for lt in range(y.shape[1] // LANES):
        cols = slice(LANES * lt, LANES * (lt + 1))
        y_scr[lt, pl.ds(0, half, stride=2), :] = y[0:half, cols]
        y_scr[lt, pl.ds(1, half, stride=2), :] = y[half:tm, cols]
        outs.append(y_scr[lt])
    y = x1[HALO:HALO + tm] + jnp.concatenate(outs, axis=1)
    o_ref[0] = _rms(y, fn_ref[...]) if final else y


def _ffn(x, oa, ob, w_out, g, w_up, conv_w, conv_b, w_down, fn, final):
    B, S, D = x.shape
    tm = TM_FFN
    assert S % tm == 0 and tm % HALO == 0 and w_down.shape[0] % LANES == 0 and FFN_CHUNK % LANES == 0
    hb = tm // HALO
    nhb = S // HALO

    def prev_blk(t):
        return jnp.maximum(t * hb - 1, 0)

    def next_blk(t):
        return jnp.minimum((t + 1) * hb, nhb - 1)

    def resident(a):
        return pl.BlockSpec(a.shape, lambda b, t: (0,) * a.ndim, pipeline_mode=pl.Buffered(1))

    return pl.pallas_call(
        functools.partial(_ffn_body, final=final),
        grid=(B, S // tm),
        in_specs=[
            pl.BlockSpec((1, tm, D), lambda b, t: (b, t, 0)),
            pl.BlockSpec((1, HALO, D), lambda b, t: (b, prev_blk(t), 0)),
            pl.BlockSpec((1, HALO, D), lambda b, t: (b, next_blk(t), 0)),
            pl.BlockSpec((1, N_PAIRS_A, tm, LANES), lambda b, t: (b, 0, t, 0)),
            pl.BlockSpec((1, N_PAIRS_A, HALO, LANES), lambda b, t: (b, 0, prev_blk(t), 0)),
            pl.BlockSpec((1, N_PAIRS_A, HALO, LANES), lambda b, t: (b, 0, next_blk(t), 0)),
            pl.BlockSpec((1, tm, B_WIDTH), lambda b, t: (b, t, 0)),
            pl.BlockSpec((1, HALO, B_WIDTH), lambda b, t: (b, prev_blk(t), 0)),
            pl.BlockSpec((1, HALO, B_WIDTH), lambda b, t: (b, next_blk(t), 0)),
            resident(w_out), resident(g), resident(w_up), resident(conv_w), resident(conv_b),
            resident(w_down), resident(fn),
        ],
        out_specs=pl.BlockSpec((1, tm, D), lambda b, t: (b, t, 0)),
        out_shape=jax.ShapeDtypeStruct((B, S, D), F32),
        scratch_shapes=[pltpu.VMEM((tm + 2 * HALO, A_WIDTH + B_WIDTH), BF16),
                        pltpu.VMEM((4 * (FFN_CHUNK // LANES), tm + 2 * HALO, LANES), F32),
                        pltpu.VMEM((D // LANES, tm, LANES), F32)],
        compiler_params=pltpu.CompilerParams(
            dimension_semantics=("arbitrary", "arbitrary"), vmem_limit_bytes=VMEM_LIMIT),
        name="ffn",
    )(x, x, x, oa, oa, oa, ob, ob, ob, w_out, g, w_up, conv_w, conv_b, w_down, fn)


def _rope_tables(S):
    rows = S // GRID_W
    row = jnp.repeat(jnp.arange(rows), GRID_W).astype(F32)
    col = jnp.tile(jnp.arange(GRID_W), rows).astype(F32)
    n = HEAD_DIM // 4
    inv = ROPE_THETA ** (-jnp.arange(n, dtype=F32) / n)
    ang = jnp.concatenate([row[:, None] * inv, col[:, None] * inv], axis=-1)
    lane = jnp.arange(LANES)
    pair = (lane % HEAD_DIM) // 2
    sign = jnp.where(lane % 2 == 0, -1.0, 1.0).astype(F32)
    return jnp.cos(ang)[:, pair], jnp.sin(ang)[:, pair] * sign


def _qa_column_order():
    c = jnp.arange(A_WIDTH)
    tile, half, dim = c // LANES, (c % LANES) // HEAD_DIM, c % HEAD_DIM
    return (tile + (A_HEADS // A_KV_HEADS) * half) * HEAD_DIM + dim


def _trunk(x, p):
    depth = p["w_in"].shape[0]
    for l in range(depth):
        qa, ka, vat, qb, *kvs = _in_proj(x, p["attn_norm"][l], p["w_in"][l], p["gq"][l], p["gk"][l], p["cos"], p["sin"])
        oa = _mix_a(qa, ka, vat)
        ob = _mix_b(qb, kvs, p["bias"])
        x = _ffn(x, oa, ob, p["w_out"][l], p["ffn_norm"][l], p["w_up"][l], p["conv_w"][l], p["conv_b"][l],
                 p["w_down"][l], p["final_norm"], final=(l == depth - 1))
    return x


def kernel(x_prompt, x_sample, attn_norm, w_in, q_norm, k_norm, rel_bias, w_out, ffn_norm, w_up, conv_w, conv_b, w_down, final_norm):
    depth, d_model, _ = w_in.shape
    order = _qa_column_order()
    w_in_p = jnp.concatenate([w_in[:, :, order], w_in[:, :, A_WIDTH:]], axis=-1).astype(BF16)
    w_out_p = jnp.concatenate([w_out[:, order, :], w_out[:, A_WIDTH:, :]], axis=1).astype(BF16)
    params = {
        "attn_norm": attn_norm.reshape(depth, 1, d_model),
        "w_in": w_in_p,
        "gq": (jnp.tile(q_norm, (1, 2)) * Q_SCALE).reshape(depth, 1, LANES),
        "gk": jnp.tile(k_norm, (1, 2)).reshape(depth, 1, LANES),
        "w_out": w_out_p,
        "ffn_norm": ffn_norm.reshape(depth, 1, d_model),
        "w_up": w_up.astype(BF16),
        "conv_w": conv_w,
        "conv_b": conv_b.reshape(depth, 1, -1),
        "w_down": w_down.astype(BF16),
        "final_norm": final_norm.reshape(1, d_model),
        "bias": _bias_tiles(rel_bias),
    }
    outs = []
    for x in (x_prompt, x_sample):
        cos2, sin2 = _rope_tables(x.shape[1])
        outs.append(_trunk(x, dict(params, cos=cos2, sin=sin2)))
    return tuple(outs)
```

```python
import functools
import math

import jax
import jax.numpy as jnp
from jax import lax
from jax.experimental import pallas as pl
from jax.experimental.pallas import tpu as pltpu

F32 = jnp.float32
BF16 = jnp.bfloat16

HEAD_DIM = 64
A_HEADS = 8
A_KV_HEADS = 2
B_HEADS = 8
GRID_W = 64
ROPE_THETA = 10000.0
DILATED_BRANCHES = ((128, 1), (512, 4), (2048, 16))
N_BUCKETS = 32
REL_MAX_DIST = 1024
CONV_WIDTH = 3
EPS = 1e-6
NEG_INF = -1e30
ATTN_SCALE = HEAD_DIM ** -0.5
LOG2E = math.log2(math.e)
Q_SCALE = ATTN_SCALE * LOG2E

LANES = 128
A_WIDTH = A_HEADS * HEAD_DIM
AKV_WIDTH = A_KV_HEADS * HEAD_DIM
B_WIDTH = B_HEADS * HEAD_DIM
N_PAIRS_A = A_WIDTH // LANES
N_PAIRS_B = B_WIDTH // LANES

DILATIONS = tuple(d for _, d in DILATED_BRANCHES)
HALF = DILATED_BRANCHES[0][0] // (2 * DILATED_BRANCHES[0][1])
assert all(w // (2 * d) == HALF for w, d in DILATED_BRANCHES)
QC = 128
KW = QC + 2 * HALF
N_VARIANTS = 3

TM_PROJ = 512
TQ_A = 256
TM_FFN = 512
FFN_CHUNK = 1024
HALO = 16
ONES_ROWS = 16
VMEM_LIMIT = 56 * 1024 * 1024


def _rms(x, g):
    ms = jnp.mean(x * x, axis=-1, keepdims=True)
    return x * lax.rsqrt(ms + EPS) * g


def _bias_body(tab_ref, out_ref):
    br = pl.program_id(0)
    var = pl.program_id(1)
    dil = jnp.where(br == 0, DILATIONS[0], jnp.where(br == 1, DILATIONS[1], DILATIONS[2]))
    i = lax.broadcasted_iota(jnp.int32, (QC, KW), 0)
    jj = lax.broadcasted_iota(jnp.int32, (QC, KW), 1)
    delta = jj - var * HALF - i
    rel = delta * dil
    nb = N_BUCKETS // 2
    max_exact = nb // 2
    n = jnp.abs(rel)
    large = max_exact + (jnp.log(jnp.maximum(n, 1).astype(F32) / max_exact)
                         / math.log(REL_MAX_DIST / max_exact) * (nb - max_exact)).astype(jnp.int32)
    large = jnp.minimum(large, nb - 1)
    bucket = jnp.where(rel > 0, nb, 0) + jnp.where(n < max_exact, n, large)
    band = jnp.abs(delta) <= HALF
    for h in range(B_HEADS):
        acc = jnp.zeros((QC, KW), F32)
        for b in range(N_BUCKETS):
            acc = jnp.where(bucket == b, tab_ref[b, h], acc)
        out_ref[0, 0, h] = jnp.where(band, acc * LOG2E, NEG_INF)


def _bias_tiles(rel_bias):
    nbr = len(DILATIONS)
    return pl.pallas_call(
        _bias_body,
        grid=(nbr, N_VARIANTS),
        in_specs=[pl.BlockSpec(memory_space=pltpu.SMEM)],
        out_specs=pl.BlockSpec((1, 1, B_HEADS, QC, KW), lambda b, v: (b, v, 0, 0, 0)),
        out_shape=jax.ShapeDtypeStruct((nbr, N_VARIANTS, B_HEADS, QC, KW), F32),
        name="bias_tiles",
    )(rel_bias)


def _inproj_body(x_ref, g_ref, w_ref, gq_ref, gk_ref, cos_ref, sin_ref,
                 qa_ref, ka_ref, vat_ref, qb_ref, *rest):
    kv_refs, kv_scr, kv4_scr = rest[:-2], rest[-2], rest[-1]
    assert DILATIONS == (1, 4, 16)
    h = _rms(x_ref[0], g_ref[...]).astype(BF16)
    lane = lax.broadcasted_iota(jnp.int32, (1, LANES), 1)
    lo = lane < HEAD_DIM
    even = (lane & 1) == 0
    cos = cos_ref[...]
    sin = sin_ref[...]

    def norm_rope(blk, gain):
        sq = blk * blk
        s_lo = jnp.sum(jnp.where(lo, sq, 0.0), axis=-1, keepdims=True)
        s_hi = jnp.sum(jnp.where(lo, 0.0, sq), axis=-1, keepdims=True)
        r = jnp.where(lo, lax.rsqrt(s_lo * (1.0 / HEAD_DIM) + EPS), lax.rsqrt(s_hi * (1.0 / HEAD_DIM) + EPS))
        y = blk * r * gain
        partner = jnp.where(even, pltpu.roll(y, LANES - 1, 1), pltpu.roll(y, 1, 1))
        return y * cos + partner * sin

    na = A_WIDTH + 2 * AKV_WIDTH
    pa = jnp.dot(h, w_ref[:, 0:na], preferred_element_type=F32)
    for i in range(N_PAIRS_A):
        qa_ref[0, i] = norm_rope(pa[:, LANES * i:LANES * (i + 1)], gq_ref[...]).astype(BF16)
    ka_ref[0] = norm_rope(pa[:, A_WIDTH:A_WIDTH + AKV_WIDTH], gk_ref[...]).astype(BF16)
    vat_ref[0, 0, 0:AKV_WIDTH] = pa[:, A_WIDTH + AKV_WIDTH:na].T.astype(BF16)
    vat_ref[0, 0, AKV_WIDTH:AKV_WIDTH + ONES_ROWS] = jnp.ones((ONES_ROWS, x_ref.shape[1]), BF16)
    qb_ref[0] = jnp.dot(h, w_ref[:, na:na + B_WIDTH], preferred_element_type=F32) * Q_SCALE
    tm = x_ref.shape[1]
    step = 2 * LANES
    for c0 in range(0, 2 * B_WIDTH, step):
        pkv = jnp.dot(h, w_ref[:, na + B_WIDTH + c0:na + B_WIDTH + c0 + step], preferred_element_type=F32)
        for half in range(step // LANES):
            ct = c0 // LANES + half
            cols = slice(LANES * ct, LANES * (ct + 1))
            tile = pkv[:, LANES * half:LANES * (half + 1)]
            kv_scr[ct] = tile
            kv1_ref, kv4_ref, kv16_ref = kv_refs
            kv1_ref[0, 0, :, cols] = tile.astype(BF16)
            for r in range(4):
                g4 = kv_scr[ct, pl.ds(r, tm // 4, stride=4), :]
                kv4_ref[0, r, :, cols] = g4.astype(BF16)
                kv4_scr[ct, r * (tm // 4):(r + 1) * (tm // 4)] = g4
            for r in range(16):
                g16 = kv4_scr[ct, pl.ds((r % 4) * (tm // 4) + r // 4, tm // 16, stride=4), :]
                kv16_ref[0, r, :, cols] = g16.astype(BF16)


def _in_proj(x, g, w, gq, gk, cos2, sin2):
    B, S, D = x.shape
    tm = TM_PROJ
    nw = w.shape[1]
    return pl.pallas_call(
        _inproj_body,
        grid=(B, S // tm),
        in_specs=[
            pl.BlockSpec((1, tm, D), lambda b, t: (b, t, 0)),
            pl.BlockSpec((1, D), lambda b, t: (0, 0)),
            pl.BlockSpec((D, nw), lambda b, t: (0, 0)),
            pl.BlockSpec((1, LANES), lambda b, t: (0, 0)),
            pl.BlockSpec((1, LANES), lambda b, t: (0, 0)),
            pl.BlockSpec((tm, LANES), lambda b, t: (t, 0)),
            pl.BlockSpec((tm, LANES), lambda b, t: (t, 0)),
        ],
        out_specs=[
            pl.BlockSpec((1, N_PAIRS_A, tm, LANES), lambda b, t: (b, 0, t, 0)),
            pl.BlockSpec((1, tm, AKV_WIDTH), lambda b, t: (b, t, 0)),
            pl.BlockSpec((1, 1, AKV_WIDTH + ONES_ROWS, tm), lambda b, t: (b, t, 0, 0)),
            pl.BlockSpec((1, tm, B_WIDTH), lambda b, t: (b, t, 0)),
        ] + [pl.BlockSpec((1, d, tm // d, 2 * B_WIDTH), lambda b, t: (b, 0, t, 0)) for d in DILATIONS],
        out_shape=[
            jax.ShapeDtypeStruct((B, N_PAIRS_A, S, LANES), BF16),
            jax.ShapeDtypeStruct((B, S, AKV_WIDTH), BF16),
            jax.ShapeDtypeStruct((B, S // tm, AKV_WIDTH + ONES_ROWS, tm), BF16),
            jax.ShapeDtypeStruct((B, S, B_WIDTH), F32),
        ] + [jax.ShapeDtypeStruct((B, d, S // d, 2 * B_WIDTH), BF16) for d in DILATIONS],
        scratch_shapes=[pltpu.VMEM((2 * N_PAIRS_B, tm, LANES), F32)] * 2,
        compiler_params=pltpu.CompilerParams(
            dimension_semantics=("arbitrary", "arbitrary"), vmem_limit_bytes=VMEM_LIMIT),
        name="in_proj",
    )(x, g, w, gq, gk, cos2, sin2)


def _mixa_body(q_ref, k_ref, vt_ref, o_ref, st_a, st_b, acc_a, acc_b, q2_scr):
    S = k_ref.shape[1]
    tq = TQ_A
    kc = vt_ref.shape[3]
    n_units = (S // tq) * N_PAIRS_A
    assert n_units % 2 == 0 and n_units >= 4
    pair_shift = N_PAIRS_A.bit_length() - 1
    lane = lax.broadcasted_iota(jnp.int32, (1, LANES), 1)
    lo = lane < HEAD_DIM

    def finalize(u, acc):
        t = lax.shift_right_logical(u, pair_shift)
        i = u & (N_PAIRS_A - 1)
        ot = acc[...]
        ot = ot[0:AKV_WIDTH] / ot[AKV_WIDTH:AKV_WIDTH + 1]
        oblk = jnp.concatenate([ot[0:HEAD_DIM, 0:tq], ot[HEAD_DIM:2 * HEAD_DIM, tq:2 * tq]], axis=0)
        o_ref[0, i, pl.ds(pl.multiple_of(t * tq, tq), tq), :] = oblk.T.astype(BF16)

    def stage(s, m_prev, st_cur, st_prv, acc_w, acc_r, scores=True, softmax=True, fin=True):
        if fin:
            finalize(s - 2, acc_r)
        if scores:
            t = lax.shift_right_logical(s, pair_shift)
            i = s & (N_PAIRS_A - 1)
            qb = q_ref[0, i, pl.ds(pl.multiple_of(t * tq, tq), tq), :]
            zero = jnp.zeros_like(qb)
            q2_scr[...] = jnp.concatenate([jnp.where(lo, qb, zero), jnp.where(lo, zero, qb)], axis=0)
        if softmax:
            acc_w[...] = jnp.zeros(acc_w.shape, F32)

        def chunk(c, mx):
            rows = pl.ds(pl.multiple_of(c * kc, kc), kc)
            if softmax:
                e = jnp.exp2(st_prv[rows, :] - m_prev).astype(BF16)
            if scores:
                sn = lax.dot_general(k_ref[0, rows, :], q2_scr[...], (((1,), (1,)), ((), ())),
                                     preferred_element_type=F32)
                st_cur[rows, :] = sn
                mx = jnp.maximum(mx, jnp.max(sn, axis=0, keepdims=True))
            if softmax:
                acc_w[...] += jnp.dot(vt_ref[0, c], e, preferred_element_type=F32)
            return mx

        return lax.fori_loop(0, S // kc, chunk, jnp.full((1, 2 * tq), NEG_INF, F32), unroll=True)

    m = stage(0, None, st_a, None, None, None, softmax=False, fin=False)
    m = stage(1, m, st_b, st_a, acc_b, None, fin=False)

    def two_stages(j, m):
        m = stage(2 * j, m, st_a, st_b, acc_a, acc_b)
        return stage(2 * j + 1, m, st_b, st_a, acc_b, acc_a)

    m = lax.fori_loop(1, n_units // 2, two_stages, m, unroll=2)
    stage(n_units, m, None, st_b, acc_a, acc_b, scores=False)
    finalize(n_units - 1, acc_a)


def _mix_a(qa, ka, vat):
    B, _, S, _ = qa.shape
    tq = TQ_A
    _, nkc, vrows, kc = vat.shape
    return pl.pallas_call(
        _mixa_body,
        grid=(B,),
        in_specs=[
            pl.BlockSpec((1, N_PAIRS_A, S, LANES), lambda b: (b, 0, 0, 0)),
            pl.BlockSpec((1, S, AKV_WIDTH), lambda b: (b, 0, 0)),
            pl.BlockSpec((1, nkc, vrows, kc), lambda b: (b, 0, 0, 0)),
        ],
        out_specs=pl.BlockSpec((1, N_PAIRS_A, S, LANES), lambda b: (b, 0, 0, 0)),
        out_shape=jax.ShapeDtypeStruct((B, N_PAIRS_A, S, LANES), BF16),
        scratch_shapes=[pltpu.VMEM((S, 2 * tq), F32)] * 2 + [pltpu.VMEM((vrows, 2 * tq), F32)] * 2
        + [pltpu.VMEM((2 * tq, LANES), BF16)],
        compiler_params=pltpu.CompilerParams(
            dimension_semantics=("arbitrary",), vmem_limit_bytes=VMEM_LIMIT),
        name="mix_a",
    )(qa, ka, vat)


def _mixb_body(q_ref, *rest):
    nbr = len(DILATIONS)
    kv_refs = rest[:2 * nbr]
    bm_ref, o_ref = rest[2 * nbr:2 * nbr + 2]
    state = rest[2 * nbr + 2:]
    S = q_ref.shape[1]
    lane = lax.broadcasted_iota(jnp.int32, (1, LANES), 1)
    lo = lane < HEAD_DIM
    ones = jnp.ones((KW, LANES), BF16)
    order = sorted(range(nbr), key=lambda bi: -DILATIONS[bi])
    assert DILATIONS[order[-1]] == 1
    assert all(DILATIONS[order[i]] % DILATIONS[order[i + 1]] == 0 for i in range(nbr - 1))

    for slot, bi in enumerate(order):
        d = DILATIONS[bi]
        k_ref, v_ref = kv_refs[2 * bi], kv_refs[2 * bi + 1]
        L = S // d
        nc = L // QC
        shift = nc.bit_length() - 1
        first, last = slot == 0, slot == nbr - 1
        src = None if first else state[3 * ((slot - 1) % 2):3 * ((slot - 1) % 2) + 3]
        dst = None if last else state[3 * (slot % 2):3 * (slot % 2) + 3]
        dn = None if last else DILATIONS[order[slot + 1]]

        def body(it, carry, bi=bi, d=d, L=L, nc=nc, shift=shift, k_ref=k_ref, v_ref=v_ref,
                 first=first, last=last, src=src, dst=dst, dn=dn):
            r = lax.shift_right_logical(it, shift)
            c = it & (nc - 1)
            j0 = c * QC
            ws = pl.multiple_of(jnp.clip(j0 - HALF, 0, L - KW), HALF)
            var = jnp.where(c == 0, 0, jnp.where(c == nc - 1, 2, 1))
            trow = pl.ds(pl.multiple_of(j0, QC), QC) if d == 1 else pl.ds(r + d * j0, QC, stride=d)
            qc = q_ref[0, trow, :].astype(BF16)
            kw = k_ref[0, r, pl.ds(ws, KW), :]
            vw = jnp.concatenate([v_ref[0, r, pl.ds(ws, KW), :], ones], axis=1)
            zero = jnp.zeros_like(qc)
            q2 = jnp.concatenate([jnp.where(lo, qc, zero), jnp.where(lo, zero, qc)], axis=0)
            s = lax.dot_general(q2, kw, (((1,), (1,)), ((), ())), preferred_element_type=F32)
            s = s + bm_ref[bi, var].reshape(2 * QC, KW)
            m = jnp.max(s, axis=-1, keepdims=True)
            e = jnp.exp2(s - m).astype(BF16)
            ox = jnp.dot(e, vw, preferred_element_type=F32)
            o = jnp.where(lo, ox[0:QC, 0:LANES], ox[QC:2 * QC, 0:LANES])
            den = jnp.where(lo, ox[0:QC, LANES:2 * LANES], ox[QC:2 * QC, LANES:2 * LANES])
            mb = jnp.where(lo, m[0:QC], m[QC:2 * QC])
            if not first:
                grow = pl.ds(pl.multiple_of(r * L + j0, QC), QC)
                o_c, m_c, den_c = src[0][grow, :], src[1][grow, :], src[2][grow, :]
                mm = jnp.maximum(mb, m_c)
                w_own, w_c = jnp.exp2(mb - mm), jnp.exp2(m_c - mm)
                o = w_own * o + w_c * o_c
                den = w_own * den + w_c * den_c
                mb = mm
            if last:
                o_ref[0, trow, :] = (o / den).astype(BF16)
            else:
                start = (r & (dn - 1)) * (S // dn) + lax.shift_right_logical(r, dn.bit_length() - 1) + (d // dn) * j0
                drow = pl.ds(start, QC, stride=d // dn)
                dst[0][drow, :] = o
                dst[1][drow, :] = mb
                dst[2][drow, :] = den
            return carry

        lax.fori_loop(0, d * nc, body, 0, unroll=True)


def _mix_b(qb, kvs, bmt):
    B, S, _ = qb.shape
    nbr = len(DILATIONS)
    assert all((S // d) % QC == 0 and S // d >= KW and ((S // d // QC) & (S // d // QC - 1)) == 0 for d in DILATIONS)
    kv_specs = []
    for d in DILATIONS:
        kv_specs.append(pl.BlockSpec((1, d, S // d, LANES), lambda p, b: (b, 0, 0, p)))
        kv_specs.append(pl.BlockSpec((1, d, S // d, LANES), lambda p, b: (b, 0, 0, N_PAIRS_B + p)))
    return pl.pallas_call(
        _mixb_body,
        grid=(N_PAIRS_B, B),
        in_specs=[pl.BlockSpec((1, S, LANES), lambda p, b: (b, 0, p))] + kv_specs + [
            pl.BlockSpec((nbr, N_VARIANTS, 2, QC, KW), lambda p, b: (0, 0, p, 0, 0)),
        ],
        out_specs=pl.BlockSpec((1, S, LANES), lambda p, b: (b, 0, p)),
        out_shape=jax.ShapeDtypeStruct((B, S, B_WIDTH), BF16),
        scratch_shapes=[pltpu.VMEM((S, LANES), F32)] * 6,
        compiler_params=pltpu.CompilerParams(
            dimension_semantics=("arbitrary", "arbitrary"), vmem_limit_bytes=VMEM_LIMIT),
        name="mix_b",
    )(qb, *[kv for kv in kvs for _ in range(2)], bmt)


def _ffn_body(x_ref, xp_ref, xn_ref, oa_ref, oap_ref, oan_ref, ob_ref, obp_ref, obn_ref,
              wo_ref, g_ref, wup_ref, cw_ref, cb_ref, wd_ref, fn_ref, o_ref, o_scr, up_scr, y_scr, *, final):
    t = pl.program_id(1)
    nt = pl.num_programs(1)
    tm = x_ref.shape[1]
    rows = tm + 2 * HALO
    dff = wd_ref.shape[0]

    for i in range(N_PAIRS_A + 1):
        cols = slice(LANES * i, LANES * (i + 1)) if i < N_PAIRS_A else slice(A_WIDTH, A_WIDTH + B_WIDTH)
        o_scr[0:HALO, cols] = oap_ref[0, i] if i < N_PAIRS_A else obp_ref[0]
        o_scr[HALO:HALO + tm, cols] = oa_ref[0, i] if i < N_PAIRS_A else ob_ref[0]
        o_scr[HALO + tm:rows, cols] = oan_ref[0, i] if i < N_PAIRS_A else obn_ref[0]
    x_ext = jnp.concatenate([xp_ref[0], x_ref[0], xn_ref[0]], axis=0)
    x1 = x_ext + jnp.dot(o_scr[...], wo_ref[...], preferred_element_type=F32)
    rid = lax.broadcasted_iota(jnp.int32, (rows, 1), 0)
    inside = jnp.logical_and(jnp.logical_or(t > 0, rid >= HALO), jnp.logical_or(t < nt - 1, rid < HALO + tm))
    h = jnp.where(inside, _rms(x1, g_ref[...]), 0.0).astype(BF16)

    half = tm // 2
    nlt = FFN_CHUNK // LANES
    y = None
    for ci, c0 in enumerate(range(0, dff, FFN_CHUNK)):
        cw = min(FFN_CHUNK, dff - c0)

        def conv(off, gv):
            up = jnp.dot(h, wup_ref[:, off:off + cw], preferred_element_type=F32)
            tiles = []
            for lt in range(cw // LANES):
                buf = up_scr.at[(ci % 2) * 2 * nlt + gv * nlt + lt]
                buf[...] = up[:, LANES * lt:LANES * (lt + 1)]
                win = [buf[pl.ds(HALO - 1 + k, half, stride=2), :] for k in range(4)]
                prev = jnp.concatenate([win[0], win[1]], axis=0)
                mid = jnp.concatenate([win[1], win[2]], axis=0)
                nxt = jnp.concatenate([win[2], win[3]], axis=0)
                cols = slice(off + LANES * lt, off + LANES * (lt + 1))
                tiles.append(cb_ref[:, cols] + prev * cw_ref[0:1, cols] + mid * cw_ref[1:2, cols]
                             + nxt * cw_ref[2:3, cols])
            return jnp.concatenate(tiles, axis=1)

        ug = conv(c0, 0)
        uv = conv(dff + c0, 1)
        act = (ug * (1.0 / (1.0 + jnp.exp(-ug))) * uv).astype(BF16)
        part = jnp.dot(act, wd_ref[c0:c0 + cw, :], preferred_element_type=F32)
        y = part if y is None else y + part
    outs = []
    for lt in range(y.shape[1] // LANES):
        cols = slice(LANES * lt, LANES * (lt + 1))
        y_scr[lt, pl.ds(0, half, stride=2), :] = y[0:half, cols]
        y_scr[lt, pl.ds(1, half, stride=2), :] = y[half:tm, cols]
        outs.append(y_scr[lt])
    y = x1[HALO:HALO + tm] + jnp.concatenate(outs, axis=1)
    o_ref[0] = _rms(y, fn_ref[...]) if final else y


def _ffn(x, oa, ob, w_out, g, w_up, conv_w, conv_b, w_down, fn, final):
    B, S, D = x.shape
    tm = TM_FFN
    assert S % tm == 0 and tm % HALO == 0 and w_down.shape[0] % LANES == 0 and FFN_CHUNK % LANES == 0
    hb = tm // HALO
    nhb = S // HALO

    def prev_blk(t):
        return jnp.maximum(t * hb - 1, 0)

    def next_blk(t):
        return jnp.minimum((t + 1) * hb, nhb - 1)

    def resident(a):
        return pl.BlockSpec(a.shape, lambda b, t: (0,) * a.ndim, pipeline_mode=pl.Buffered(1))

    return pl.pallas_call(
        functools.partial(_ffn_body, final=final),
        grid=(B, S // tm),
        in_specs=[
            pl.BlockSpec((1, tm, D), lambda b, t: (b, t, 0)),
            pl.BlockSpec((1, HALO, D), lambda b, t: (b, prev_blk(t), 0)),
            pl.BlockSpec((1, HALO, D), lambda b, t: (b, next_blk(t), 0)),
            pl.BlockSpec((1, N_PAIRS_A, tm, LANES), lambda b, t: (b, 0, t, 0)),
            pl.BlockSpec((1, N_PAIRS_A, HALO, LANES), lambda b, t: (b, 0, prev_blk(t), 0)),
            pl.BlockSpec((1, N_PAIRS_A, HALO, LANES), lambda b, t: (b, 0, next_blk(t), 0)),
            pl.BlockSpec((1, tm, B_WIDTH), lambda b, t: (b, t, 0)),
            pl.BlockSpec((1, HALO, B_WIDTH), lambda b, t: (b, prev_blk(t), 0)),
            pl.BlockSpec((1, HALO, B_WIDTH), lambda b, t: (b, next_blk(t), 0)),
            resident(w_out), resident(g), resident(w_up), resident(conv_w), resident(conv_b),
            resident(w_down), resident(fn),
        ],
        out_specs=pl.BlockSpec((1, tm, D), lambda b, t: (b, t, 0)),
        out_shape=jax.ShapeDtypeStruct((B, S, D), F32),
        scratch_shapes=[pltpu.VMEM((tm + 2 * HALO, A_WIDTH + B_WIDTH), BF16),
                        pltpu.VMEM((4 * (FFN_CHUNK // LANES), tm + 2 * HALO, LANES), F32),
                        pltpu.VMEM((D // LANES, tm, LANES), F32)],
        compiler_params=pltpu.CompilerParams(
            dimension_semantics=("arbitrary", "arbitrary"), vmem_limit_bytes=VMEM_LIMIT),
        name="ffn",
    )(x, x, x, oa, oa, oa, ob, ob, ob, w_out, g, w_up, conv_w, conv_b, w_down, fn)


def _rope_tables(S):
    rows = S // GRID_W
    row = jnp.repeat(jnp.arange(rows), GRID_W).astype(F32)
    col = jnp.tile(jnp.arange(GRID_W), rows).astype(F32)
    n = HEAD_DIM // 4
    inv = ROPE_THETA ** (-jnp.arange(n, dtype=F32) / n)
    ang = jnp.concatenate([row[:, None] * inv, col[:, None] * inv], axis=-1)
    lane = jnp.arange(LANES)
    pair = (lane % HEAD_DIM) // 2
    sign = jnp.where(lane % 2 == 0, -1.0, 1.0).astype(F32)
    return jnp.cos(ang)[:, pair], jnp.sin(ang)[:, pair] * sign


def _qa_column_order():
    c = jnp.arange(A_WIDTH)
    tile, half, dim = c // LANES, (c % LANES) // HEAD_DIM, c % HEAD_DIM
    return (tile + (A_HEADS // A_KV_HEADS) * half) * HEAD_DIM + dim


def _trunk(x, p):
    depth = p["w_in"].shape[0]
    for l in range(depth):
        qa, ka, vat, qb, *kvs = _in_proj(x, p["attn_norm"][l], p["w_in"][l], p["gq"][l], p["gk"][l], p["cos"], p["sin"])
        oa = _mix_a(qa, ka, vat)
        ob = _mix_b(qb, kvs, p["bias"])
        x = _ffn(x, oa, ob, p["w_out"][l], p["ffn_norm"][l], p["w_up"][l], p["conv_w"][l], p["conv_b"][l],
                 p["w_down"][l], p["final_norm"], final=(l == depth - 1))
    return x


def kernel(x_prompt, x_sample, attn_norm, w_in, q_norm, k_norm, rel_bias, w_out, ffn_norm, w_up, conv_w, conv_b, w_down, final_norm):
    depth, d_model, _ = w_in.shape
    order = _qa_column_order()
    w_in_p = jnp.concatenate([w_in[:, :, order], w_in[:, :, A_WIDTH:]], axis=-1).astype(BF16)
    w_out_p = jnp.concatenate([w_out[:, order, :], w_out[:, A_WIDTH:, :]], axis=1).astype(BF16)
    params = {
        "attn_norm": attn_norm.reshape(depth, 1, d_model),
        "w_in": w_in_p,
        "gq": (jnp.tile(q_norm, (1, 2)) * Q_SCALE).reshape(depth, 1, LANES),
        "gk": jnp.tile(k_norm, (1, 2)).reshape(depth, 1, LANES),
        "w_out": w_out_p,
        "ffn_norm": ffn_norm.reshape(depth, 1, d_model),
        "w_up": w_up.astype(BF16),
        "conv_w": conv_w,
        "conv_b": conv_b.reshape(depth, 1, -1),
        "w_down": w_down.astype(BF16),
        "final_norm": final_norm.reshape(1, d_model),
        "bias": _bias_tiles(rel_bias),
    }
    outs = []
    for x in (x_prompt, x_sample):
        cos2, sin2 = _rope_tables(x.shape[1])
        outs.append(_trunk(x, dict(params, cos=cos2, sin=sin2)))
    return tuple(outs)
```

```python
import functools
import math

import jax
import jax.numpy as jnp
from jax import lax
from jax.experimental import pallas as pl
from jax.experimental.pallas import tpu as pltpu

F32 = jnp.float32
BF16 = jnp.bfloat16

HEAD_DIM = 64
A_HEADS = 8
A_KV_HEADS = 2
B_HEADS = 8
GRID_W = 64
ROPE_THETA = 10000.0
DILATED_BRANCHES = ((128, 1), (512, 4), (2048, 16))
N_BUCKETS = 32
REL_MAX_DIST = 1024
CONV_WIDTH = 3
EPS = 1e-6
NEG_INF = -1e30
ATTN_SCALE = HEAD_DIM ** -0.5
LOG2E = math.log2(math.e)
Q_SCALE = ATTN_SCALE * LOG2E

LANES = 128
A_WIDTH = A_HEADS * HEAD_DIM
AKV_WIDTH = A_KV_HEADS * HEAD_DIM
B_WIDTH = B_HEADS * HEAD_DIM
N_PAIRS_A = A_WIDTH // LANES
N_PAIRS_B = B_WIDTH // LANES

DILATIONS = tuple(d for _, d in DILATED_BRANCHES)
HALF = DILATED_BRANCHES[0][0] // (2 * DILATED_BRANCHES[0][1])
assert all(w // (2 * d) == HALF for w, d in DILATED_BRANCHES)
QC = 128
KW = QC + 2 * HALF
N_VARIANTS = 3

TM_PROJ = 512
TQ_A = 256
TM_FFN = 512
MXU_COLS = 256
HALO = 16
ONES_ROWS = 16
VMEM_LIMIT = 56 * 1024 * 1024


def _rms(x, g):
    ms = jnp.mean(x * x, axis=-1, keepdims=True)
    return x * lax.rsqrt(ms + EPS) * g


def _bias_body(tab_ref, out_ref):
    br = pl.program_id(0)
    var = pl.program_id(1)
    dil = jnp.where(br == 0, DILATIONS[0], jnp.where(br == 1, DILATIONS[1], DILATIONS[2]))
    i = lax.broadcasted_iota(jnp.int32, (QC, KW), 0)
    jj = lax.broadcasted_iota(jnp.int32, (QC, KW), 1)
    delta = jj - var * HALF - i
    rel = delta * dil
    nb = N_BUCKETS // 2
    max_exact = nb // 2
    n = jnp.abs(rel)
    large = max_exact + (jnp.log(jnp.maximum(n, 1).astype(F32) / max_exact)
                         / math.log(REL_MAX_DIST / max_exact) * (nb - max_exact)).astype(jnp.int32)
    large = jnp.minimum(large, nb - 1)
    bucket = jnp.where(rel > 0, nb, 0) + jnp.where(n < max_exact, n, large)
    band = jnp.abs(delta) <= HALF
    for h in range(B_HEADS):
        acc = jnp.zeros((QC, KW), F32)
        for b in range(N_BUCKETS):
            acc = jnp.where(bucket == b, tab_ref[b, h], acc)
        out_ref[0, 0, h] = jnp.where(band, acc * LOG2E, NEG_INF)


def _bias_tiles(rel_bias):
    nbr = len(DILATIONS)
    return pl.pallas_call(
        _bias_body,
        grid=(nbr, N_VARIANTS),
        in_specs=[pl.BlockSpec(memory_space=pltpu.SMEM)],
        out_specs=pl.BlockSpec((1, 1, B_HEADS, QC, KW), lambda b, v: (b, v, 0, 0, 0)),
        out_shape=jax.ShapeDtypeStruct((nbr, N_VARIANTS, B_HEADS, QC, KW), F32),
        name="bias_tiles",
    )(rel_bias)


def _inproj_body(x_ref, g_ref, w_ref, gq_ref, gk_ref, cos_ref, sin_ref,
                 qa_ref, ka_ref, vat_ref, qb_ref, *rest):
    kv_refs, kv_scr, kv4_scr = rest[:-2], rest[-2], rest[-1]
    assert DILATIONS == (1, 4, 16)
    h = _rms(x_ref[0], g_ref[...]).astype(BF16)
    lane = lax.broadcasted_iota(jnp.int32, (1, LANES), 1)
    lo = lane < HEAD_DIM
    even = (lane & 1) == 0
    cos = cos_ref[...]
    sin = sin_ref[...]

    def norm_rope(blk, gain):
        sq = blk * blk
        s_lo = jnp.sum(jnp.where(lo, sq, 0.0), axis=-1, keepdims=True)
        s_hi = jnp.sum(jnp.where(lo, 0.0, sq), axis=-1, keepdims=True)
        r = jnp.where(lo, lax.rsqrt(s_lo * (1.0 / HEAD_DIM) + EPS), lax.rsqrt(s_hi * (1.0 / HEAD_DIM) + EPS))
        y = blk * r * gain
        partner = jnp.where(even, pltpu.roll(y, LANES - 1, 1), pltpu.roll(y, 1, 1))
        return y * cos + partner * sin

    na = A_WIDTH + 2 * AKV_WIDTH
    pa = jnp.dot(h, w_ref[:, 0:na], preferred_element_type=F32)
    for i in range(N_PAIRS_A):
        qa_ref[0, i] = norm_rope(pa[:, LANES * i:LANES * (i + 1)], gq_ref[...]).astype(BF16)
    ka_ref[0] = norm_rope(pa[:, A_WIDTH:A_WIDTH + AKV_WIDTH], gk_ref[...]).astype(BF16)
    vat_ref[0, 0, 0:AKV_WIDTH] = pa[:, A_WIDTH + AKV_WIDTH:na].T.astype(BF16)
    vat_ref[0, 0, AKV_WIDTH:AKV_WIDTH + ONES_ROWS] = jnp.ones((ONES_ROWS, x_ref.shape[1]), BF16)
    qb_ref[0] = jnp.dot(h, w_ref[:, na:na + B_WIDTH], preferred_element_type=F32) * Q_SCALE
    tm = x_ref.shape[1]
    step = 2 * LANES
    for c0 in range(0, 2 * B_WIDTH, step):
        pkv = jnp.dot(h, w_ref[:, na + B_WIDTH + c0:na + B_WIDTH + c0 + step], preferred_element_type=F32)
        for half in range(step // LANES):
            ct = c0 // LANES + half
            cols = slice(LANES * ct, LANES * (ct + 1))
            tile = pkv[:, LANES * half:LANES * (half + 1)]
            kv_scr[ct] = tile
            kv1_ref, kv4_ref, kv16_ref = kv_refs
            kv1_ref[0, 0, :, cols] = tile.astype(BF16)
            for r in range(4):
                g4 = kv_scr[ct, pl.ds(r, tm // 4, stride=4), :]
                kv4_ref[0, r, :, cols] = g4.astype(BF16)
                kv4_scr[ct, r * (tm // 4):(r + 1) * (tm // 4)] = g4
            for r in range(16):
                g16 = kv4_scr[ct, pl.ds((r % 4) * (tm // 4) + r // 4, tm // 16, stride=4), :]
                kv16_ref[0, r, :, cols] = g16.astype(BF16)


def _in_proj(x, g, w, gq, gk, cos2, sin2):
    B, S, D = x.shape
    tm = TM_PROJ
    nw = w.shape[1]
    return pl.pallas_call(
        _inproj_body,
        grid=(B, S // tm),
        in_specs=[
            pl.BlockSpec((1, tm, D), lambda b, t: (b, t, 0)),
            pl.BlockSpec((1, D), lambda b, t: (0, 0)),
            pl.BlockSpec((D, nw), lambda b, t: (0, 0)),
            pl.BlockSpec((1, LANES), lambda b, t: (0, 0)),
            pl.BlockSpec((1, LANES), lambda b, t: (0, 0)),
            pl.BlockSpec((tm, LANES), lambda b, t: (t, 0)),
            pl.BlockSpec((tm, LANES), lambda b, t: (t, 0)),
        ],
        out_specs=[
            pl.BlockSpec((1, N_PAIRS_A, tm, LANES), lambda b, t: (b, 0, t, 0)),
            pl.BlockSpec((1, tm, AKV_WIDTH), lambda b, t: (b, t, 0)),
            pl.BlockSpec((1, 1, AKV_WIDTH + ONES_ROWS, tm), lambda b, t: (b, t, 0, 0)),
            pl.BlockSpec((1, tm, B_WIDTH), lambda b, t: (b, t, 0)),
        ] + [pl.BlockSpec((1, d, tm // d, 2 * B_WIDTH), lambda b, t: (b, 0, t, 0)) for d in DILATIONS],
        out_shape=[
            jax.ShapeDtypeStruct((B, N_PAIRS_A, S, LANES), BF16),
            jax.ShapeDtypeStruct((B, S, AKV_WIDTH), BF16),
            jax.ShapeDtypeStruct((B, S // tm, AKV_WIDTH + ONES_ROWS, tm), BF16),
            jax.ShapeDtypeStruct((B, S, B_WIDTH), F32),
        ] + [jax.ShapeDtypeStruct((B, d, S // d, 2 * B_WIDTH), BF16) for d in DILATIONS],
        scratch_shapes=[pltpu.VMEM((2 * N_PAIRS_B, tm, LANES), F32)] * 2,
        compiler_params=pltpu.CompilerParams(
            dimension_semantics=("arbitrary", "arbitrary"), vmem_limit_bytes=VMEM_LIMIT),
        name="in_proj",
    )(x, g, w, gq, gk, cos2, sin2)


def _mixa_body(q_ref, k_ref, vt_ref, o_ref, st_a, st_b, acc_a, acc_b, q2_scr):
    S = k_ref.shape[1]
    tq = TQ_A
    kc = vt_ref.shape[3]
    n_units = (S // tq) * N_PAIRS_A
    assert n_units % 2 == 0 and n_units >= 4
    pair_shift = N_PAIRS_A.bit_length() - 1
    lane = lax.broadcasted_iota(jnp.int32, (1, LANES), 1)
    lo = lane < HEAD_DIM

    def finalize(u, acc):
        t = lax.shift_right_logical(u, pair_shift)
        i = u & (N_PAIRS_A - 1)
        ot = acc[...]
        ot = ot[0:AKV_WIDTH] / ot[AKV_WIDTH:AKV_WIDTH + 1]
        oblk = jnp.concatenate([ot[0:HEAD_DIM, 0:tq], ot[HEAD_DIM:2 * HEAD_DIM, tq:2 * tq]], axis=0)
        o_ref[0, i, pl.ds(pl.multiple_of(t * tq, tq), tq), :] = oblk.T.astype(BF16)

    def stage(s, m_prev, st_cur, st_prv, acc_w, acc_r, scores=True, softmax=True, fin=True):
        if fin:
            finalize(s - 2, acc_r)
        if scores:
            t = lax.shift_right_logical(s, pair_shift)
            i = s & (N_PAIRS_A - 1)
            qb = q_ref[0, i, pl.ds(pl.multiple_of(t * tq, tq), tq), :]
            zero = jnp.zeros_like(qb)
            q2_scr[...] = jnp.concatenate([jnp.where(lo, qb, zero), jnp.where(lo, zero, qb)], axis=0)
        if softmax:
            acc_w[...] = jnp.zeros(acc_w.shape, F32)

        def chunk(c, mx):
            rows = pl.ds(pl.multiple_of(c * kc, kc), kc)
            if softmax:
                e = jnp.exp2(st_prv[rows, :] - m_prev).astype(BF16)
            if scores:
                sn = lax.dot_general(k_ref[0, rows, :], q2_scr[...], (((1,), (1,)), ((), ())),
                                     preferred_element_type=F32)
                st_cur[rows, :] = sn
                mx = jnp.maximum(mx, jnp.max(sn, axis=0, keepdims=True))
            if softmax:
                acc_w[...] += jnp.dot(vt_ref[0, c], e, preferred_element_type=F32)
            return mx

        return lax.fori_loop(0, S // kc, chunk, jnp.full((1, 2 * tq), NEG_INF, F32), unroll=True)

    m = stage(0, None, st_a, None, None, None, softmax=False, fin=False)
    m = stage(1, m, st_b, st_a, acc_b, None, fin=False)

    def two_stages(j, m):
        m = stage(2 * j, m, st_a, st_b, acc_a, acc_b)
        return stage(2 * j + 1, m, st_b, st_a, acc_b, acc_a)

    m = lax.fori_loop(1, n_units // 2, two_stages, m, unroll=2)
    stage(n_units, m, None, st_b, acc_a, acc_b, scores=False)
    finalize(n_units - 1, acc_a)


def _mix_a(qa, ka, vat):
    B, _, S, _ = qa.shape
    tq = TQ_A
    _, nkc, vrows, kc = vat.shape
    return pl.pallas_call(
        _mixa_body,
        grid=(B,),
        in_specs=[
            pl.BlockSpec((1, N_PAIRS_A, S, LANES), lambda b: (b, 0, 0, 0)),
            pl.BlockSpec((1, S, AKV_WIDTH), lambda b: (b, 0, 0)),
            pl.BlockSpec((1, nkc, vrows, kc), lambda b: (b, 0, 0, 0)),
        ],
        out_specs=pl.BlockSpec((1, N_PAIRS_A, S, LANES), lambda b: (b, 0, 0, 0)),
        out_shape=jax.ShapeDtypeStruct((B, N_PAIRS_A, S, LANES), BF16),
        scratch_shapes=[pltpu.VMEM((S, 2 * tq), F32)] * 2 + [pltpu.VMEM((vrows, 2 * tq), F32)] * 2
        + [pltpu.VMEM((2 * tq, LANES), BF16)],
        compiler_params=pltpu.CompilerParams(
            dimension_semantics=("arbitrary",), vmem_limit_bytes=VMEM_LIMIT),
        name="mix_a",
    )(qa, ka, vat)


def _mixb_body(q_ref, *rest):
    nbr = len(DILATIONS)
    kv_refs = rest[:2 * nbr]
    bm_ref, o_ref = rest[2 * nbr:2 * nbr + 2]
    state = rest[2 * nbr + 2:]
    S = q_ref.shape[1]
    lane = lax.broadcasted_iota(jnp.int32, (1, LANES), 1)
    lo = lane < HEAD_DIM
    ones = jnp.ones((KW, LANES), BF16)
    order = sorted(range(nbr), key=lambda bi: -DILATIONS[bi])
    assert DILATIONS[order[-1]] == 1
    assert all(DILATIONS[order[i]] % DILATIONS[order[i + 1]] == 0 for i in range(nbr - 1))

    for slot, bi in enumerate(order):
        d = DILATIONS[bi]
        k_ref, v_ref = kv_refs[2 * bi], kv_refs[2 * bi + 1]
        L = S // d
        nc = L // QC
        shift = nc.bit_length() - 1
        first, last = slot == 0, slot == nbr - 1
        src = None if first else state[3 * ((slot - 1) % 2):3 * ((slot - 1) % 2) + 3]
        dst = None if last else state[3 * (slot % 2):3 * (slot % 2) + 3]
        dn = None if last else DILATIONS[order[slot + 1]]

        def body(it, carry, bi=bi, d=d, L=L, nc=nc, shift=shift, k_ref=k_ref, v_ref=v_ref,
                 first=first, last=last, src=src, dst=dst, dn=dn):
            r = lax.shift_right_logical(it, shift)
            c = it & (nc - 1)
            j0 = c * QC
            ws = pl.multiple_of(jnp.clip(j0 - HALF, 0, L - KW), HALF)
            var = jnp.where(c == 0, 0, jnp.where(c == nc - 1, 2, 1))
            trow = pl.ds(pl.multiple_of(j0, QC), QC) if d == 1 else pl.ds(r + d * j0, QC, stride=d)
            qc = q_ref[0, trow, :].astype(BF16)
            kw = k_ref[0, r, pl.ds(ws, KW), :]
            vw = jnp.concatenate([v_ref[0, r, pl.ds(ws, KW), :], ones], axis=1)
            zero = jnp.zeros_like(qc)
            q2 = jnp.concatenate([jnp.where(lo, qc, zero), jnp.where(lo, zero, qc)], axis=0)
            s = lax.dot_general(q2, kw, (((1,), (1,)), ((), ())), preferred_element_type=F32)
            s = s + bm_ref[bi, var].reshape(2 * QC, KW)
            m = jnp.max(s, axis=-1, keepdims=True)
            e = jnp.exp2(s - m).astype(BF16)
            ox = jnp.dot(e, vw, preferred_element_type=F32)
            o = jnp.where(lo, ox[0:QC, 0:LANES], ox[QC:2 * QC, 0:LANES])
            den = jnp.where(lo, ox[0:QC, LANES:2 * LANES], ox[QC:2 * QC, LANES:2 * LANES])
            mb = jnp.where(lo, m[0:QC], m[QC:2 * QC])
            if not first:
                grow = pl.ds(pl.multiple_of(r * L + j0, QC), QC)
                o_c, m_c, den_c = src[0][grow, :], src[1][grow, :], src[2][grow, :]
                mm = jnp.maximum(mb, m_c)
                w_own, w_c = jnp.exp2(mb - mm), jnp.exp2(m_c - mm)
                o = w_own * o + w_c * o_c
                den = w_own * den + w_c * den_c
                mb = mm
            if last:
                o_ref[0, trow, :] = (o / den).astype(BF16)
            else:
                start = (r & (dn - 1)) * (S // dn) + lax.shift_right_logical(r, dn.bit_length() - 1) + (d // dn) * j0
                drow = pl.ds(start, QC, stride=d // dn)
                dst[0][drow, :] = o
                dst[1][drow, :] = mb
                dst[2][drow, :] = den
            return carry

        lax.fori_loop(0, d * nc, body, 0, unroll=True)


def _mix_b(qb, kvs, bmt):
    B, S, _ = qb.shape
    nbr = len(DILATIONS)
    assert all((S // d) % QC == 0 and S // d >= KW and ((S // d // QC) & (S // d // QC - 1)) == 0 for d in DILATIONS)
    kv_specs = []
    for d in DILATIONS:
        kv_specs.append(pl.BlockSpec((1, d, S // d, LANES), lambda p, b: (b, 0, 0, p)))
        kv_specs.append(pl.BlockSpec((1, d, S // d, LANES), lambda p, b: (b, 0, 0, N_PAIRS_B + p)))
    return pl.pallas_call(
        _mixb_body,
        grid=(N_PAIRS_B, B),
        in_specs=[pl.BlockSpec((1, S, LANES), lambda p, b: (b, 0, p))] + kv_specs + [
            pl.BlockSpec((nbr, N_VARIANTS, 2, QC, KW), lambda p, b: (0, 0, p, 0, 0)),
        ],
        out_specs=pl.BlockSpec((1, S, LANES), lambda p, b: (b, 0, p)),
        out_shape=jax.ShapeDtypeStruct((B, S, B_WIDTH), BF16),
        scratch_shapes=[pltpu.VMEM((S, LANES), F32)] * 6,
        compiler_params=pltpu.CompilerParams(
            dimension_semantics=("arbitrary", "arbitrary"), vmem_limit_bytes=VMEM_LIMIT),
        name="mix_b",
    )(qb, *[kv for kv in kvs for _ in range(2)], bmt)


def _ffn_chunks(dff):
    first = -(-(dff // 2) // MXU_COLS) * MXU_COLS
    return (first, dff - first) if first < dff else (dff,)


def _ffn_body(x_ref, xp_ref, xn_ref, oa_ref, oap_ref, oan_ref, ob_ref, obp_ref, obn_ref,
              wo_ref, g_ref, wup_ref, cw_ref, cb_ref, wd_ref, fn_ref, o_ref, o_scr, up_scr, y_scr, *, final):
    t = pl.program_id(1)
    nt = pl.num_programs(1)
    tm = x_ref.shape[1]
    rows = tm + 2 * HALO
    dff = wd_ref.shape[0]

    for i in range(N_PAIRS_A + 1):
        cols = slice(LANES * i, LANES * (i + 1)) if i < N_PAIRS_A else slice(A_WIDTH, A_WIDTH + B_WIDTH)
        o_scr[0:HALO, cols] = oap_ref[0, i] if i < N_PAIRS_A else obp_ref[0]
        o_scr[HALO:HALO + tm, cols] = oa_ref[0, i] if i < N_PAIRS_A else ob_ref[0]
        o_scr[HALO + tm:rows, cols] = oan_ref[0, i] if i < N_PAIRS_A else obn_ref[0]
    x_ext = jnp.concatenate([xp_ref[0], x_ref[0], xn_ref[0]], axis=0)
    x1 = x_ext + jnp.dot(o_scr[...], wo_ref[...], preferred_element_type=F32)
    rid = lax.broadcasted_iota(jnp.int32, (rows, 1), 0)
    inside = jnp.logical_and(jnp.logical_or(t > 0, rid >= HALO), jnp.logical_or(t < nt - 1, rid < HALO + tm))
    h = jnp.where(inside, _rms(x1, g_ref[...]), 0.0).astype(BF16)

    half = tm // 2
    chunks = _ffn_chunks(dff)
    nlt = max(chunks) // LANES
    y = None
    for ci, cw in enumerate(chunks):
        c0 = sum(chunks[:ci])

        def conv(off, gv):
            up = jnp.dot(h, wup_ref[:, off:off + cw], preferred_element_type=F32)
            tiles = []
            for lt in range(cw // LANES):
                buf = up_scr.at[(ci % 2) * 2 * nlt + gv * nlt + lt]
                buf[...] = up[:, LANES * lt:LANES * (lt + 1)]
                win = [buf[pl.ds(HALO - 1 + k, half, stride=2), :] for k in range(4)]
                prev = jnp.concatenate([win[0], win[1]], axis=0)
                mid = jnp.concatenate([win[1], win[2]], axis=0)
                nxt = jnp.concatenate([win[2], win[3]], axis=0)
                cols = slice(off + LANES * lt, off + LANES * (lt + 1))
                tiles.append(cb_ref[:, cols] + prev * cw_ref[0:1, cols] + mid * cw_ref[1:2, cols]
                             + nxt * cw_ref[2:3, cols])
            return jnp.concatenate(tiles, axis=1)

        ug = conv(c0, 0)
        uv = conv(dff + c0, 1)
        act = (ug * (1.0 / (1.0 + jnp.exp(-ug))) * uv).astype(BF16)
        part = jnp.dot(act, wd_ref[c0:c0 + cw, :], preferred_element_type=F32)
        y = part if y is None else y + part
    outs = []
    for lt in range(y.shape[1] // LANES):
        cols = slice(LANES * lt, LANES * (lt + 1))
        y_scr[lt, pl.ds(0, half, stride=2), :] = y[0:half, cols]
        y_scr[lt, pl.ds(1, half, stride=2), :] = y[half:tm, cols]
        outs.append(y_scr[lt])
    y = x1[HALO:HALO + tm] + jnp.concatenate(outs, axis=1)
    o_ref[0] = _rms(y, fn_ref[...]) if final else y


def _ffn(x, oa, ob, w_out, g, w_up, conv_w, conv_b, w_down, fn, final):
    B, S, D = x.shape
    tm = TM_FFN
    assert S % tm == 0 and tm % HALO == 0 and w_down.shape[0] % LANES == 0
    hb = tm // HALO
    nhb = S // HALO

    def prev_blk(t):
        return jnp.maximum(t * hb - 1, 0)

    def next_blk(t):
        return jnp.minimum((t + 1) * hb, nhb - 1)

    def resident(a):
        return pl.BlockSpec(a.shape, lambda b, t: (0,) * a.ndim, pipeline_mode=pl.Buffered(1))

    return pl.pallas_call(
        functools.partial(_ffn_body, final=final),
        grid=(B, S // tm),
        in_specs=[
            pl.BlockSpec((1, tm, D), lambda b, t: (b, t, 0)),
            pl.BlockSpec((1, HALO, D), lambda b, t: (b, prev_blk(t), 0)),
            pl.BlockSpec((1, HALO, D), lambda b, t: (b, next_blk(t), 0)),
            pl.BlockSpec((1, N_PAIRS_A, tm, LANES), lambda b, t: (b, 0, t, 0)),
            pl.BlockSpec((1, N_PAIRS_A, HALO, LANES), lambda b, t: (b, 0, prev_blk(t), 0)),
            pl.BlockSpec((1, N_PAIRS_A, HALO, LANES), lambda b, t: (b, 0, next_blk(t), 0)),
            pl.BlockSpec((1, tm, B_WIDTH), lambda b, t: (b, t, 0)),
            pl.BlockSpec((1, HALO, B_WIDTH), lambda b, t: (b, prev_blk(t), 0)),
            pl.BlockSpec((1, HALO, B_WIDTH), lambda b, t: (b, next_blk(t), 0)),
            resident(w_out), resident(g), resident(w_up), resident(conv_w), resident(conv_b),
            resident(w_down), resident(fn),
        ],
        out_specs=pl.BlockSpec((1, tm, D), lambda b, t: (b, t, 0)),
        out_shape=jax.ShapeDtypeStruct((B, S, D), F32),
        scratch_shapes=[pltpu.VMEM((tm + 2 * HALO, A_WIDTH + B_WIDTH), BF16),
                        pltpu.VMEM((4 * (max(_ffn_chunks(w_down.shape[0])) // LANES), tm + 2 * HALO, LANES), F32),
                        pltpu.VMEM((D // LANES, tm, LANES), F32)],
        compiler_params=pltpu.CompilerParams(
            dimension_semantics=("arbitrary", "arbitrary"), vmem_limit_bytes=VMEM_LIMIT),
        name="ffn",
    )(x, x, x, oa, oa, oa, ob, ob, ob, w_out, g, w_up, conv_w, conv_b, w_down, fn)


def _rope_tables(S):
    rows = S // GRID_W
    row = jnp.repeat(jnp.arange(rows), GRID_W).astype(F32)
    col = jnp.tile(jnp.arange(GRID_W), rows).astype(F32)
    n = HEAD_DIM // 4
    inv = ROPE_THETA ** (-jnp.arange(n, dtype=F32) / n)
    ang = jnp.concatenate([row[:, None] * inv, col[:, None] * inv], axis=-1)
    lane = jnp.arange(LANES)
    pair = (lane % HEAD_DIM) // 2
    sign = jnp.where(lane % 2 == 0, -1.0, 1.0).astype(F32)
    return jnp.cos(ang)[:, pair], jnp.sin(ang)[:, pair] * sign


def _qa_column_order():
    c = jnp.arange(A_WIDTH)
    tile, half, dim = c // LANES, (c % LANES) // HEAD_DIM, c % HEAD_DIM
    return (tile + (A_HEADS // A_KV_HEADS) * half) * HEAD_DIM + dim


def _trunk(x, p):
    depth = p["w_in"].shape[0]
    for l in range(depth):
        qa, ka, vat, qb, *kvs = _in_proj(x, p["attn_norm"][l], p["w_in"][l], p["gq"][l], p["gk"][l], p["cos"], p["sin"])
        oa = _mix_a(qa, ka, vat)
        ob = _mix_b(qb, kvs, p["bias"])
        x = _ffn(x, oa, ob, p["w_out"][l], p["ffn_norm"][l], p["w_up"][l], p["conv_w"][l], p["conv_b"][l],
                 p["w_down"][l], p["final_norm"], final=(l == depth - 1))
    return x


def kernel(x_prompt, x_sample, attn_norm, w_in, q_norm, k_norm, rel_bias, w_out, ffn_norm, w_up, conv_w, conv_b, w_down, final_norm):
    depth, d_model, _ = w_in.shape
    order = _qa_column_order()
    w_in_p = jnp.concatenate([w_in[:, :, order], w_in[:, :, A_WIDTH:]], axis=-1).astype(BF16)
    w_out_p = jnp.concatenate([w_out[:, order, :], w_out[:, A_WIDTH:, :]], axis=1).astype(BF16)
    params = {
        "attn_norm": attn_norm.reshape(depth, 1, d_model),
        "w_in": w_in_p,
        "gq": (jnp.tile(q_norm, (1, 2)) * Q_SCALE).reshape(depth, 1, LANES),
        "gk": jnp.tile(k_norm, (1, 2)).reshape(depth, 1, LANES),
        "w_out": w_out_p,
        "ffn_norm": ffn_norm.reshape(depth, 1, d_model),
        "w_up": w_up.astype(BF16),
        "conv_w": conv_w,
        "conv_b": conv_b.reshape(depth, 1, -1),
        "w_down": w_down.astype(BF16),
        "final_norm": final_norm.reshape(1, d_model),
        "bias": _bias_tiles(rel_bias),
    }
    outs = []
    for x in (x_prompt, x_sample):
        cos2, sin2 = _rope_tables(x.shape[1])
        outs.append(_trunk(x, dict(params, cos=cos2, sin=sin2)))
    return tuple(outs)
```

```python
import functools
import math

import jax
import jax.numpy as jnp
from jax import lax
from jax.experimental import pallas as pl
from jax.experimental.pallas import tpu as pltpu

F32 = jnp.float32
BF16 = jnp.bfloat16

HEAD_DIM = 64
A_HEADS = 8
A_KV_HEADS = 2
B_HEADS = 8
GRID_W = 64
ROPE_THETA = 10000.0
DILATED_BRANCHES = ((128, 1), (512, 4), (2048, 16))
N_BUCKETS = 32
REL_MAX_DIST = 1024
CONV_WIDTH = 3
EPS = 1e-6
NEG_INF = -1e30
ATTN_SCALE = HEAD_DIM ** -0.5
LOG2E = math.log2(math.e)
Q_SCALE = ATTN_SCALE * LOG2E

LANES = 128
A_WIDTH = A_HEADS * HEAD_DIM
AKV_WIDTH = A_KV_HEADS * HEAD_DIM
B_WIDTH = B_HEADS * HEAD_DIM
N_PAIRS_A = A_WIDTH // LANES
N_PAIRS_B = B_WIDTH // LANES

DILATIONS = tuple(d for _, d in DILATED_BRANCHES)
HALF = DILATED_BRANCHES[0][0] // (2 * DILATED_BRANCHES[0][1])
assert all(w // (2 * d) == HALF for w, d in DILATED_BRANCHES)
QC = 128
KW = QC + 2 * HALF
N_VARIANTS = 3

TM_PROJ = 512
TQ_A = 256
TM_FFN = 512
MXU_COLS = 256
HALO = 16
ONES_ROWS = 16
VMEM_LIMIT = 56 * 1024 * 1024


def _rms(x, g):
    ms = jnp.mean(x * x, axis=-1, keepdims=True)
    return x * lax.rsqrt(ms + EPS) * g


def _bias_body(tab_ref, out_ref):
    br = pl.program_id(0)
    var = pl.program_id(1)
    dil = jnp.where(br == 0, DILATIONS[0], jnp.where(br == 1, DILATIONS[1], DILATIONS[2]))
    i = lax.broadcasted_iota(jnp.int32, (QC, KW), 0)
    jj = lax.broadcasted_iota(jnp.int32, (QC, KW), 1)
    delta = jj - var * HALF - i
    rel = delta * dil
    nb = N_BUCKETS // 2
    max_exact = nb // 2
    n = jnp.abs(rel)
    large = max_exact + (jnp.log(jnp.maximum(n, 1).astype(F32) / max_exact)
                         / math.log(REL_MAX_DIST / max_exact) * (nb - max_exact)).astype(jnp.int32)
    large = jnp.minimum(large, nb - 1)
    bucket = jnp.where(rel > 0, nb, 0) + jnp.where(n < max_exact, n, large)
    band = jnp.abs(delta) <= HALF
    for h in range(B_HEADS):
        acc = jnp.zeros((QC, KW), F32)
        for b in range(N_BUCKETS):
            acc = jnp.where(bucket == b, tab_ref[b, h], acc)
        out_ref[0, 0, h] = jnp.where(band, acc * LOG2E, NEG_INF)


def _bias_tiles(rel_bias):
    nbr = len(DILATIONS)
    return pl.pallas_call(
        _bias_body,
        grid=(nbr, N_VARIANTS),
        in_specs=[pl.BlockSpec(memory_space=pltpu.SMEM)],
        out_specs=pl.BlockSpec((1, 1, B_HEADS, QC, KW), lambda b, v: (b, v, 0, 0, 0)),
        out_shape=jax.ShapeDtypeStruct((nbr, N_VARIANTS, B_HEADS, QC, KW), F32),
        name="bias_tiles",
    )(rel_bias)


def _inproj_body(x_ref, g_ref, w_ref, gq_ref, gk_ref, cos_ref, sin_ref,
                 qa_ref, ka_ref, vat_ref, qb_ref, *rest):
    kv_refs, kv_scr, kv4_scr = rest[:-2], rest[-2], rest[-1]
    assert DILATIONS == (1, 4, 16)
    h = _rms(x_ref[0], g_ref[...]).astype(BF16)
    lane = lax.broadcasted_iota(jnp.int32, (1, LANES), 1)
    lo = lane < HEAD_DIM
    even = (lane & 1) == 0
    cos = cos_ref[...]
    sin = sin_ref[...]

    def norm_rope(blk, gain):
        sq = blk * blk
        s_lo = jnp.sum(jnp.where(lo, sq, 0.0), axis=-1, keepdims=True)
        s_hi = jnp.sum(jnp.where(lo, 0.0, sq), axis=-1, keepdims=True)
        r = jnp.where(lo, lax.rsqrt(s_lo * (1.0 / HEAD_DIM) + EPS), lax.rsqrt(s_hi * (1.0 / HEAD_DIM) + EPS))
        y = blk * r * gain
        partner = jnp.where(even, pltpu.roll(y, LANES - 1, 1), pltpu.roll(y, 1, 1))
        return y * cos + partner * sin

    na = A_WIDTH + 2 * AKV_WIDTH
    pa = jnp.dot(h, w_ref[:, 0:na], preferred_element_type=F32)
    for i in range(N_PAIRS_A):
        qa_ref[0, i] = norm_rope(pa[:, LANES * i:LANES * (i + 1)], gq_ref[...]).astype(BF16)
    ka_ref[0] = norm_rope(pa[:, A_WIDTH:A_WIDTH + AKV_WIDTH], gk_ref[...]).astype(BF16)
    vat_ref[0, 0, 0:AKV_WIDTH] = pa[:, A_WIDTH + AKV_WIDTH:na].T.astype(BF16)
    vat_ref[0, 0, AKV_WIDTH:AKV_WIDTH + ONES_ROWS] = jnp.ones((ONES_ROWS, x_ref.shape[1]), BF16)
    qb_ref[0] = jnp.dot(h, w_ref[:, na:na + B_WIDTH], preferred_element_type=F32) * Q_SCALE
    tm = x_ref.shape[1]
    step = 2 * LANES
    for c0 in range(0, 2 * B_WIDTH, step):
        pkv = jnp.dot(h, w_ref[:, na + B_WIDTH + c0:na + B_WIDTH + c0 + step], preferred_element_type=F32)
        for half in range(step // LANES):
            ct = c0 // LANES + half
            cols = slice(LANES * ct, LANES * (ct + 1))
            tile = pkv[:, LANES * half:LANES * (half + 1)]
            kv_scr[ct] = tile
            kv1_ref, kv4_ref, kv16_ref = kv_refs
            kv1_ref[0, 0, :, cols] = tile.astype(BF16)
            for r in range(4):
                g4 = kv_scr[ct, pl.ds(r, tm // 4, stride=4), :]
                kv4_ref[0, r, :, cols] = g4.astype(BF16)
                kv4_scr[ct, r * (tm // 4):(r + 1) * (tm // 4)] = g4
            for r in range(16):
                g16 = kv4_scr[ct, pl.ds((r % 4) * (tm // 4) + r // 4, tm // 16, stride=4), :]
                kv16_ref[0, r, :, cols] = g16.astype(BF16)


def _in_proj(x, g, w, gq, gk, cos2, sin2):
    B, S, D = x.shape
    tm = TM_PROJ
    nw = w.shape[1]
    return pl.pallas_call(
        _inproj_body,
        grid=(B, S // tm),
        in_specs=[
            pl.BlockSpec((1, tm, D), lambda b, t: (b, t, 0)),
            pl.BlockSpec((1, D), lambda b, t: (0, 0)),
            pl.BlockSpec((D, nw), lambda b, t: (0, 0)),
            pl.BlockSpec((1, LANES), lambda b, t: (0, 0)),
            pl.BlockSpec((1, LANES), lambda b, t: (0, 0)),
            pl.BlockSpec((tm, LANES), lambda b, t: (t, 0)),
            pl.BlockSpec((tm, LANES), lambda b, t: (t, 0)),
        ],
        out_specs=[
            pl.BlockSpec((1, N_PAIRS_A, tm, LANES), lambda b, t: (b, 0, t, 0)),
            pl.BlockSpec((1, tm, AKV_WIDTH), lambda b, t: (b, t, 0)),
            pl.BlockSpec((1, 1, AKV_WIDTH + ONES_ROWS, tm), lambda b, t: (b, t, 0, 0)),
            pl.BlockSpec((1, tm, B_WIDTH), lambda b, t: (b, t, 0)),
        ] + [pl.BlockSpec((1, d, tm // d, 2 * B_WIDTH), lambda b, t: (b, 0, t, 0)) for d in DILATIONS],
        out_shape=[
            jax.ShapeDtypeStruct((B, N_PAIRS_A, S, LANES), BF16),
            jax.ShapeDtypeStruct((B, S, AKV_WIDTH), BF16),
            jax.ShapeDtypeStruct((B, S // tm, AKV_WIDTH + ONES_ROWS, tm), BF16),
            jax.ShapeDtypeStruct((B, S, B_WIDTH), F32),
        ] + [jax.ShapeDtypeStruct((B, d, S // d, 2 * B_WIDTH), BF16) for d in DILATIONS],
        scratch_shapes=[pltpu.VMEM((2 * N_PAIRS_B, tm, LANES), F32)] * 2,
        compiler_params=pltpu.CompilerParams(
            dimension_semantics=("arbitrary", "arbitrary"), vmem_limit_bytes=VMEM_LIMIT),
        name="in_proj",
    )(x, g, w, gq, gk, cos2, sin2)


def _mixa_body(q_ref, k_ref, vt_ref, o_ref, st_a, st_b, acc_a, acc_b, q2_scr):
    S = k_ref.shape[1]
    tq = TQ_A
    kc = vt_ref.shape[3]
    n_units = (S // tq) * N_PAIRS_A
    assert n_units % 2 == 0 and n_units >= 4
    pair_shift = N_PAIRS_A.bit_length() - 1
    lane = lax.broadcasted_iota(jnp.int32, (1, LANES), 1)
    lo = lane < HEAD_DIM

    def finalize(u, acc):
        t = lax.shift_right_logical(u, pair_shift)
        i = u & (N_PAIRS_A - 1)
        ot = acc[...]
        ot = ot[0:AKV_WIDTH] / ot[AKV_WIDTH:AKV_WIDTH + 1]
        oblk = jnp.concatenate([ot[0:HEAD_DIM, 0:tq], ot[HEAD_DIM:2 * HEAD_DIM, tq:2 * tq]], axis=0)
        o_ref[0, i, pl.ds(pl.multiple_of(t * tq, tq), tq), :] = oblk.T.astype(BF16)

    def stage(s, m_prev, st_cur, st_prv, acc_w, acc_r, scores=True, softmax=True, fin=True):
        if fin:
            finalize(s - 2, acc_r)
        if scores:
            t = lax.shift_right_logical(s, pair_shift)
            i = s & (N_PAIRS_A - 1)
            qb = q_ref[0, i, pl.ds(pl.multiple_of(t * tq, tq), tq), :]
            zero = jnp.zeros_like(qb)
            q2_scr[...] = jnp.concatenate([jnp.where(lo, qb, zero), jnp.where(lo, zero, qb)], axis=0)
        if softmax:
            acc_w[...] = jnp.zeros(acc_w.shape, F32)

        def chunk(c, mx):
            rows = pl.ds(pl.multiple_of(c * kc, kc), kc)
            if softmax:
                e = jnp.exp2(st_prv[rows, :] - m_prev).astype(BF16)
            if scores:
                sn = lax.dot_general(k_ref[0, rows, :], q2_scr[...], (((1,), (1,)), ((), ())),
                                     preferred_element_type=F32)
                st_cur[rows, :] = sn
                mx = jnp.maximum(mx, jnp.max(sn, axis=0, keepdims=True))
            if softmax:
                acc_w[...] += jnp.dot(vt_ref[0, c], e, preferred_element_type=F32)
            return mx

        return lax.fori_loop(0, S // kc, chunk, jnp.full((1, 2 * tq), NEG_INF, F32), unroll=True)

    m = stage(0, None, st_a, None, None, None, softmax=False, fin=False)
    m = stage(1, m, st_b, st_a, acc_b, None, fin=False)

    def two_stages(j, m):
        m = stage(2 * j, m, st_a, st_b, acc_a, acc_b)
        return stage(2 * j + 1, m, st_b, st_a, acc_b, acc_a)

    m = lax.fori_loop(1, n_units // 2, two_stages, m, unroll=3)
    stage(n_units, m, None, st_b, acc_a, acc_b, scores=False)
    finalize(n_units - 1, acc_a)


def _mix_a(qa, ka, vat):
    B, _, S, _ = qa.shape
    tq = TQ_A
    _, nkc, vrows, kc = vat.shape
    return pl.pallas_call(
        _mixa_body,
        grid=(B,),
        in_specs=[
            pl.BlockSpec((1, N_PAIRS_A, S, LANES), lambda b: (b, 0, 0, 0)),
            pl.BlockSpec((1, S, AKV_WIDTH), lambda b: (b, 0, 0)),
            pl.BlockSpec((1, nkc, vrows, kc), lambda b: (b, 0, 0, 0)),
        ],
        out_specs=pl.BlockSpec((1, N_PAIRS_A, S, LANES), lambda b: (b, 0, 0, 0)),
        out_shape=jax.ShapeDtypeStruct((B, N_PAIRS_A, S, LANES), BF16),
        scratch_shapes=[pltpu.VMEM((S, 2 * tq), F32)] * 2 + [pltpu.VMEM((vrows, 2 * tq), F32)] * 2
        + [pltpu.VMEM((2 * tq, LANES), BF16)],
        compiler_params=pltpu.CompilerParams(
            dimension_semantics=("arbitrary",), vmem_limit_bytes=VMEM_LIMIT),
        name="mix_a",
    )(qa, ka, vat)


def _mixb_body(q_ref, *rest):
    nbr = len(DILATIONS)
    kv_refs = rest[:2 * nbr]
    bm_ref, o_ref = rest[2 * nbr:2 * nbr + 2]
    state = rest[2 * nbr + 2:]
    S = q_ref.shape[1]
    lane = lax.broadcasted_iota(jnp.int32, (1, LANES), 1)
    lo = lane < HEAD_DIM
    ones = jnp.ones((KW, LANES), BF16)
    order = sorted(range(nbr), key=lambda bi: -DILATIONS[bi])
    assert DILATIONS[order[-1]] == 1
    assert all(DILATIONS[order[i]] % DILATIONS[order[i + 1]] == 0 for i in range(nbr - 1))

    for slot, bi in enumerate(order):
        d = DILATIONS[bi]
        k_ref, v_ref = kv_refs[2 * bi], kv_refs[2 * bi + 1]
        L = S // d
        nc = L // QC
        shift = nc.bit_length() - 1
        first, last = slot == 0, slot == nbr - 1
        src = None if first else state[3 * ((slot - 1) % 2):3 * ((slot - 1) % 2) + 3]
        dst = None if last else state[3 * (slot % 2):3 * (slot % 2) + 3]
        dn = None if last else DILATIONS[order[slot + 1]]

        def body(it, carry, bi=bi, d=d, L=L, nc=nc, shift=shift, k_ref=k_ref, v_ref=v_ref,
                 first=first, last=last, src=src, dst=dst, dn=dn):
            r = lax.shift_right_logical(it, shift)
            c = it & (nc - 1)
            j0 = c * QC
            ws = pl.multiple_of(jnp.clip(j0 - HALF, 0, L - KW), HALF)
            var = jnp.where(c == 0, 0, jnp.where(c == nc - 1, 2, 1))
            trow = pl.ds(pl.multiple_of(j0, QC), QC) if d == 1 else pl.ds(r + d * j0, QC, stride=d)
            qc = q_ref[0, trow, :].astype(BF16)
            kw = k_ref[0, r, pl.ds(ws, KW), :]
            vw = jnp.concatenate([v_ref[0, r, pl.ds(ws, KW), :], ones], axis=1)
            zero = jnp.zeros_like(qc)
            q2 = jnp.concatenate([jnp.where(lo, qc, zero), jnp.where(lo, zero, qc)], axis=0)
            s = lax.dot_general(q2, kw, (((1,), (1,)), ((), ())), preferred_element_type=F32)
            s = s + bm_ref[bi, var].reshape(2 * QC, KW)
            m = jnp.max(s, axis=-1, keepdims=True)
            e = jnp.exp2(s - m).astype(BF16)
            ox = jnp.dot(e, vw, preferred_element_type=F32)
            o = jnp.where(lo, ox[0:QC, 0:LANES], ox[QC:2 * QC, 0:LANES])
            den = jnp.where(lo, ox[0:QC, LANES:2 * LANES], ox[QC:2 * QC, LANES:2 * LANES])
            mb = jnp.where(lo, m[0:QC], m[QC:2 * QC])
            if not first:
                grow = pl.ds(pl.multiple_of(r * L + j0, QC), QC)
                o_c, m_c, den_c = src[0][grow, :], src[1][grow, :], src[2][grow, :]
                mm = jnp.maximum(mb, m_c)
                w_own, w_c = jnp.exp2(mb - mm), jnp.exp2(m_c - mm)
                o = w_own * o + w_c * o_c
                den = w_own * den + w_c * den_c
                mb = mm
            if last:
                o_ref[0, trow, :] = (o / den).astype(BF16)
            else:
                start = (r & (dn - 1)) * (S // dn) + lax.shift_right_logical(r, dn.bit_length() - 1) + (d // dn) * j0
                drow = pl.ds(start, QC, stride=d // dn)
                dst[0][drow, :] = o
                dst[1][drow, :] = mb
                dst[2][drow, :] = den
            return carry

        lax.fori_loop(0, d * nc, body, 0, unroll=True)


def _mix_b(qb, kvs, bmt):
    B, S, _ = qb.shape
    nbr = len(DILATIONS)
    assert all((S // d) % QC == 0 and S // d >= KW and ((S // d // QC) & (S // d // QC - 1)) == 0 for d in DILATIONS)
    kv_specs = []
    for d in DILATIONS:
        kv_specs.append(pl.BlockSpec((1, d, S // d, LANES), lambda p, b: (b, 0, 0, p)))
        kv_specs.append(pl.BlockSpec((1, d, S // d, LANES), lambda p, b: (b, 0, 0, N_PAIRS_B + p)))
    return pl.pallas_call(
        _mixb_body,
        grid=(N_PAIRS_B, B),
        in_specs=[pl.BlockSpec((1, S, LANES), lambda p, b: (b, 0, p))] + kv_specs + [
            pl.BlockSpec((nbr, N_VARIANTS, 2, QC, KW), lambda p, b: (0, 0, p, 0, 0)),
        ],
        out_specs=pl.BlockSpec((1, S, LANES), lambda p, b: (b, 0, p)),
        out_shape=jax.ShapeDtypeStruct((B, S, B_WIDTH), BF16),
        scratch_shapes=[pltpu.VMEM((S, LANES), F32)] * 6,
        compiler_params=pltpu.CompilerParams(
            dimension_semantics=("arbitrary", "arbitrary"), vmem_limit_bytes=VMEM_LIMIT),
        name="mix_b",
    )(qb, *[kv for kv in kvs for _ in range(2)], bmt)


def _ffn_chunks(dff):
    first = -(-(dff // 2) // MXU_COLS) * MXU_COLS
    return (first, dff - first) if first < dff else (dff,)


def _ffn_body(x_ref, xp_ref, xn_ref, oa_ref, oap_ref, oan_ref, ob_ref, obp_ref, obn_ref,
              wo_ref, g_ref, wup_ref, cw_ref, cb_ref, wd_ref, fn_ref, o_ref, o_scr, up_scr, y_scr, *, final):
    t = pl.program_id(1)
    nt = pl.num_programs(1)
    tm = x_ref.shape[1]
    rows = tm + 2 * HALO
    dff = wd_ref.shape[0]

    for i in range(N_PAIRS_A + 1):
        cols = slice(LANES * i, LANES * (i + 1)) if i < N_PAIRS_A else slice(A_WIDTH, A_WIDTH + B_WIDTH)
        o_scr[0:HALO, cols] = oap_ref[0, i] if i < N_PAIRS_A else obp_ref[0]
        o_scr[HALO:HALO + tm, cols] = oa_ref[0, i] if i < N_PAIRS_A else ob_ref[0]
        o_scr[HALO + tm:rows, cols] = oan_ref[0, i] if i < N_PAIRS_A else obn_ref[0]
    x_ext = jnp.concatenate([xp_ref[0], x_ref[0], xn_ref[0]], axis=0)
    x1 = x_ext + jnp.dot(o_scr[...], wo_ref[...], preferred_element_type=F32)
    rid = lax.broadcasted_iota(jnp.int32, (rows, 1), 0)
    inside = jnp.logical_and(jnp.logical_or(t > 0, rid >= HALO), jnp.logical_or(t < nt - 1, rid < HALO + tm))
    h = jnp.where(inside, _rms(x1, g_ref[...]), 0.0).astype(BF16)

    half = tm // 2
    chunks = _ffn_chunks(dff)
    nlt = max(chunks) // LANES
    y = None
    for ci, cw in enumerate(chunks):
        c0 = sum(chunks[:ci])

        def conv(off, gv):
            up = jnp.dot(h, wup_ref[:, off:off + cw], preferred_element_type=F32)
            tiles = []
            for lt in range(cw // LANES):
                buf = up_scr.at[(ci % 2) * 2 * nlt + gv * nlt + lt]
                buf[...] = up[:, LANES * lt:LANES * (lt + 1)]
                win = [buf[pl.ds(HALO - 1 + k, half, stride=2), :] for k in range(4)]
                prev = jnp.concatenate([win[0], win[1]], axis=0)
                mid = jnp.concatenate([win[1], win[2]], axis=0)
                nxt = jnp.concatenate([win[2], win[3]], axis=0)
                cols = slice(off + LANES * lt, off + LANES * (lt + 1))
                tiles.append(cb_ref[:, cols] + prev * cw_ref[0:1, cols] + mid * cw_ref[1:2, cols]
                             + nxt * cw_ref[2:3, cols])
            return jnp.concatenate(tiles, axis=1)

        ug = conv(c0, 0)
        uv = conv(dff + c0, 1)
        act = (ug * (1.0 / (1.0 + jnp.exp(-ug))) * uv).astype(BF16)
        part = jnp.dot(act, wd_ref[c0:c0 + cw, :], preferred_element_type=F32)
        y = part if y is None else y + part
    outs = []
    for lt in range(y.shape[1] // LANES):
        cols = slice(LANES * lt, LANES * (lt + 1))
        y_scr[lt, pl.ds(0, half, stride=2), :] = y[0:half, cols]
        y_scr[lt, pl.ds(1, half, stride=2), :] = y[half:tm, cols]
        outs.append(y_scr[lt])
    y = x1[HALO:HALO + tm] + jnp.concatenate(outs, axis=1)
    o_ref[0] = _rms(y, fn_ref[...]) if final else y


def _ffn(x, oa, ob, w_out, g, w_up, conv_w, conv_b, w_down, fn, final):
    B, S, D = x.shape
    tm = TM_FFN
    assert S % tm == 0 and tm % HALO == 0 and w_down.shape[0] % LANES == 0
    hb = tm // HALO
    nhb = S // HALO

    def prev_blk(t):
        return jnp.maximum(t * hb - 1, 0)

    def next_blk(t):
        return jnp.minimum((t + 1) * hb, nhb - 1)

    def resident(a):
        return pl.BlockSpec(a.shape, lambda b, t: (0,) * a.ndim, pipeline_mode=pl.Buffered(1))

    return pl.pallas_call(
        functools.partial(_ffn_body, final=final),
        grid=(B, S // tm),
        in_specs=[
            pl.BlockSpec((1, tm, D), lambda b, t: (b, t, 0)),
            pl.BlockSpec((1, HALO, D), lambda b, t: (b, prev_blk(t), 0)),
            pl.BlockSpec((1, HALO, D), lambda b, t: (b, next_blk(t), 0)),
            pl.BlockSpec((1, N_PAIRS_A, tm, LANES), lambda b, t: (b, 0, t, 0)),
            pl.BlockSpec((1, N_PAIRS_A, HALO, LANES), lambda b, t: (b, 0, prev_blk(t), 0)),
            pl.BlockSpec((1, N_PAIRS_A, HALO, LANES), lambda b, t: (b, 0, next_blk(t), 0)),
            pl.BlockSpec((1, tm, B_WIDTH), lambda b, t: (b, t, 0)),
            pl.BlockSpec((1, HALO, B_WIDTH), lambda b, t: (b, prev_blk(t), 0)),
            pl.BlockSpec((1, HALO, B_WIDTH), lambda b, t: (b, next_blk(t), 0)),
            resident(w_out), resident(g), resident(w_up), resident(conv_w), resident(conv_b),
            resident(w_down), resident(fn),
        ],
        out_specs=pl.BlockSpec((1, tm, D), lambda b, t: (b, t, 0)),
        out_shape=jax.ShapeDtypeStruct((B, S, D), F32),
        scratch_shapes=[pltpu.VMEM((tm + 2 * HALO, A_WIDTH + B_WIDTH), BF16),
                        pltpu.VMEM((4 * (max(_ffn_chunks(w_down.shape[0])) // LANES), tm + 2 * HALO, LANES), F32),
                        pltpu.VMEM((D // LANES, tm, LANES), F32)],
        compiler_params=pltpu.CompilerParams(
            dimension_semantics=("arbitrary", "arbitrary"), vmem_limit_bytes=VMEM_LIMIT),
        name="ffn",
    )(x, x, x, oa, oa, oa, ob, ob, ob, w_out, g, w_up, conv_w, conv_b, w_down, fn)


def _rope_tables(S):
    rows = S // GRID_W
    row = jnp.repeat(jnp.arange(rows), GRID_W).astype(F32)
    col = jnp.tile(jnp.arange(GRID_W), rows).astype(F32)
    n = HEAD_DIM // 4
    inv = ROPE_THETA ** (-jnp.arange(n, dtype=F32) / n)
    ang = jnp.concatenate([row[:, None] * inv, col[:, None] * inv], axis=-1)
    lane = jnp.arange(LANES)
    pair = (lane % HEAD_DIM) // 2
    sign = jnp.where(lane % 2 == 0, -1.0, 1.0).astype(F32)
    return jnp.cos(ang)[:, pair], jnp.sin(ang)[:, pair] * sign


def _qa_column_order():
    c = jnp.arange(A_WIDTH)
    tile, half, dim = c // LANES, (c % LANES) // HEAD_DIM, c % HEAD_DIM
    return (tile + (A_HEADS // A_KV_HEADS) * half) * HEAD_DIM + dim


def _trunk(x, p):
    depth = p["w_in"].shape[0]
    for l in range(depth):
        qa, ka, vat, qb, *kvs = _in_proj(x, p["attn_norm"][l], p["w_in"][l], p["gq"][l], p["gk"][l], p["cos"], p["sin"])
        oa = _mix_a(qa, ka, vat)
        ob = _mix_b(qb, kvs, p["bias"])
        x = _ffn(x, oa, ob, p["w_out"][l], p["ffn_norm"][l], p["w_up"][l], p["conv_w"][l], p["conv_b"][l],
                 p["w_down"][l], p["final_norm"], final=(l == depth - 1))
    return x


def kernel(x_prompt, x_sample, attn_norm, w_in, q_norm, k_norm, rel_bias, w_out, ffn_norm, w_up, conv_w, conv_b, w_down, final_norm):
    depth, d_model, _ = w_in.shape
    order = _qa_column_order()
    w_in_p = jnp.concatenate([w_in[:, :, order], w_in[:, :, A_WIDTH:]], axis=-1).astype(BF16)
    w_out_p = jnp.concatenate([w_out[:, order, :], w_out[:, A_WIDTH:, :]], axis=1).astype(BF16)
    params = {
        "attn_norm": attn_norm.reshape(depth, 1, d_model),
        "w_in": w_in_p,
        "gq": (jnp.tile(q_norm, (1, 2)) * Q_SCALE).reshape(depth, 1, LANES),
        "gk": jnp.tile(k_norm, (1, 2)).reshape(depth, 1, LANES),
        "w_out": w_out_p,
        "ffn_norm": ffn_norm.reshape(depth, 1, d_model),
        "w_up": w_up.astype(BF16),
        "conv_w": conv_w,
        "conv_b": conv_b.reshape(depth, 1, -1),
        "w_down": w_down.astype(BF16),
        "final_norm": final_norm.reshape(1, d_model),
        "bias": _bias_tiles(rel_bias),
    }
    outs = []
    for x in (x_prompt, x_sample):
        cos2, sin2 = _rope_tables(x.shape[1])
        outs.append(_trunk(x, dict(params, cos=cos2, sin=sin2)))
    return tuple(outs)
```
